```python
import math
import jax, jax.numpy as jnp
from jax import lax
import numpy as np

D_MODEL = 1024
BATCH = 2
SEQ = 8192
DEPTH = 4
DEC_BATCH = 32
DEC_SEQ = 2048
PAST_LEN = 128

N_MIXERS = 2
N_ATTN_LAYERS = (DEPTH + 1) // 2
N_SSM_LAYERS = DEPTH // 2
GROUP_PATTERNS = ((128, 1), (512, 4), (2048, 16))
N_ATT_GROUPS = len(GROUP_PATTERNS)
HEADS_PER_GROUP = 16
HEAD_DIM = 64
ATT_WIDTH = HEADS_PER_GROUP * HEAD_DIM
ATT_IN_COLS = N_ATT_GROUPS * 3 * ATT_WIDTH + ATT_WIDTH
ROPE_THETA = 10000.0
SSM_WIDTH = D_MODEL
SSM_GROUP = 16
SSM_GROUPS = SSM_WIDTH // SSM_GROUP
SSM_STATE = 64
SSM_CHUNK = 128
DT_MIN = 0.001
DT_MAX = 0.1
NORM_EPS = 1e-6

kernel_name = "hybrid_dilated_attn_s5_encoder"


def _rmsnorm(x, g):
    xf = x.astype(jnp.float32)
    y = xf * lax.rsqrt(jnp.mean(xf * xf, axis=-1, keepdims=True) + NORM_EPS)
    return (y * g.astype(jnp.float32)).astype(x.dtype)


def _rope_tables(seq_len):
    inv_freq = ROPE_THETA ** (-jnp.arange(0, HEAD_DIM, 2, dtype=jnp.float32) / HEAD_DIM)
    ang = jnp.arange(seq_len, dtype=jnp.float32)[:, None] * inv_freq[None, :]
    return jnp.cos(ang), jnp.sin(ang)


def _rope(t, cos, sin):
    half = HEAD_DIM // 2
    c = cos[None, :, None, :].astype(t.dtype)
    s = sin[None, :, None, :].astype(t.dtype)
    t1, t2 = t[..., :half], t[..., half:]
    return jnp.concatenate([t1 * c - t2 * s, t2 * c + t1 * s], axis=-1)


def _banded_attention(q, k, v, half):
    n, h, length, dh = q.shape
    w = half
    nb = -(-length // w)
    lp = nb * w
    qp = jnp.pad(q, ((0, 0), (0, 0), (0, lp - length), (0, 0))).reshape(n, h, nb, w, dh)
    pad_kv = ((0, 0), (0, 0), (w, lp - length + w), (0, 0))
    kp = jnp.pad(k, pad_kv).reshape(n, h, nb + 2, w, dh)
    vp = jnp.pad(v, pad_kv).reshape(n, h, nb + 2, w, dh)
    kw = jnp.concatenate([kp[:, :, 0:nb], kp[:, :, 1:nb + 1], kp[:, :, 2:nb + 2]], axis=3)
    vw = jnp.concatenate([vp[:, :, 0:nb], vp[:, :, 1:nb + 1], vp[:, :, 2:nb + 2]], axis=3)
    s = jnp.einsum('nhbid,nhbjd->nhbij', qp.astype(jnp.float32), kw.astype(jnp.float32)) * (dh ** -0.5)
    qi = jnp.arange(w)[:, None]
    kj = jnp.arange(3 * w)[None, :]
    blk = jnp.arange(nb)[:, None, None]
    key_pos = blk * w + kj - w
    valid = (jnp.abs((kj - w) - qi) <= half)[None] & (key_pos >= 0) & (key_pos < length)
    s = jnp.where(valid, s, -jnp.inf)
    m = jnp.max(s, axis=-1, keepdims=True)
    p = jnp.exp(s - m)
    den = jnp.sum(p, axis=-1, keepdims=True)
    o = jnp.einsum('nhbij,nhbjd->nhbid', p, vw.astype(jnp.float32)) / den
    lse = (m + jnp.log(den))[..., 0]
    o = o.reshape(n, h, lp, dh)[:, :, :length]
    lse = lse.reshape(n, h, lp)[:, :, :length]
    return o, lse


def _dilated_window_attention(q, k, v, dilation, half):
    b, s, h, dh = q.shape
    length = s // dilation

    def to_classes(t):
        return t.reshape(b, length, dilation, h, dh).transpose(0, 2, 3, 1, 4).reshape(b * dilation, h, length, dh)

    o, lse = _banded_attention(to_classes(q), to_classes(k), to_classes(v), half)
    o = o.reshape(b, dilation, h, length, dh).transpose(0, 3, 1, 2, 4).reshape(b, s, h, dh)
    lse = lse.reshape(b, dilation, h, length).transpose(0, 3, 1, 2).reshape(b, s, h)
    return o, lse


def _attention_mixer(hn, w_in, w_out):
    b, s, _ = hn.shape
    proj = hn @ w_in
    qkv = proj[..., :N_ATT_GROUPS * 3 * ATT_WIDTH].reshape(b, s, N_ATT_GROUPS, 3, HEADS_PER_GROUP, HEAD_DIM)
    z = proj[..., N_ATT_GROUPS * 3 * ATT_WIDTH:]
    cos, sin = _rope_tables(s)
    outs, lses = [], []
    for g, (window, dilation) in enumerate(GROUP_PATTERNS):
        q = _rope(qkv[:, :, g, 0], cos, sin)
        k = _rope(qkv[:, :, g, 1], cos, sin)
        v = qkv[:, :, g, 2]
        o, lse = _dilated_window_attention(q, k, v, dilation, window // (2 * dilation))
        outs.append(o)
        lses.append(lse)
    wts = jax.nn.softmax(jnp.stack(lses, axis=0), axis=0)
    o = jnp.einsum('gbsh,gbshd->bshd', wts, jnp.stack(outs, axis=0))
    y = o.reshape(b, s, ATT_WIDTH).astype(hn.dtype) * jax.nn.silu(z)
    return y @ w_out


def _complex_affine_combine(e1, e2):
    a1r, a1i, b1r, b1i = e1
    a2r, a2i, b2r, b2i = e2
    ar = a1r * a2r - a1i * a2i
    ai = a1r * a2i + a1i * a2r
    br = a2r * b1r - a2i * b1i + b2r
    bi = a2r * b1i + a2i * b1r + b2i
    return ar, ai, br, bi


def _ssm_direction(u, a_re, a_im, log_step, b_re, b_im, c_re, c_im):
    bsz, s, g, cg = u.shape
    p = a_re.shape[-1]
    dt = jnp.exp(log_step)[:, None]
    ldr = a_re * dt
    ldi = a_im * dt
    mag = jnp.exp(ldr)
    abar_re = mag * jnp.cos(ldi)
    abar_im = mag * jnp.sin(ldi)
    den = a_re * a_re + a_im * a_im
    nr = abar_re - 1.0
    ni = abar_im
    f_re = (nr * a_re + ni * a_im) / den
    f_im = (ni * a_re - nr * a_im) / den
    bb_re = f_re[..., None] * b_re - f_im[..., None] * b_im
    bb_im = f_re[..., None] * b_im + f_im[..., None] * b_re
    t = SSM_CHUNK
    nc = s // t
    t_idx = jnp.arange(1, t + 1, dtype=jnp.float32)[:, None, None]
    pw_mag = jnp.exp(t_idx * ldr)
    pw_re = pw_mag * jnp.cos(t_idx * ldi)
    pw_im = pw_mag * jnp.sin(t_idx * ldi)
    ar_full = jnp.broadcast_to(abar_re, (bsz, t, g, p))
    ai_full = jnp.broadcast_to(abar_im, (bsz, t, g, p))
    u_chunks = u.reshape(bsz, nc, t, g, cg).transpose(1, 0, 2, 3, 4)

    def step(carry, uc):
        h0r, h0i = carry
        bu_re = jnp.einsum('btgc,gpc->btgp', uc, bb_re)
        bu_im = jnp.einsum('btgc,gpc->btgp', uc, bb_im)
        _, _, hr, hi = lax.associative_scan(_complex_affine_combine, (ar_full, ai_full, bu_re, bu_im), axis=1)
        hr = hr + pw_re * h0r[:, None] - pw_im * h0i[:, None]
        hi = hi + pw_re * h0i[:, None] + pw_im * h0r[:, None]
        y = jnp.einsum('btgp,gcp->btgc', hr, c_re) - jnp.einsum('btgp,gcp->btgc', hi, c_im)
        return (hr[:, -1], hi[:, -1]), y

    init = (jnp.zeros((bsz, g, p), jnp.float32), jnp.zeros((bsz, g, p), jnp.float32))
    _, y = lax.scan(step, init, u_chunks)
    return y.transpose(1, 0, 2, 3, 4).reshape(bsz, s, g, cg)


def _ssm_mixer(hn, w_in, a_re, a_im, log_step, b_re, b_im, c_re, c_im, d, w_glu, b_glu, w_out):
    bsz, s, _ = hn.shape
    proj = hn @ w_in
    u = proj[..., :SSM_WIDTH]
    z = proj[..., SSM_WIDTH:]
    f32 = jnp.float32
    uf = u.astype(f32)
    ug = uf.reshape(bsz, s, SSM_GROUPS, SSM_GROUP)
    y_fwd = _ssm_direction(ug, a_re[0].astype(f32), a_im[0].astype(f32), log_step[0].astype(f32),
                           b_re[0].astype(f32), b_im[0].astype(f32), c_re[0].astype(f32), c_im[0].astype(f32))
    y_bwd = jnp.flip(_ssm_direction(jnp.flip(ug, axis=1), a_re[1].astype(f32), a_im[1].astype(f32),
                                    log_step[1].astype(f32), b_re[1].astype(f32), b_im[1].astype(f32),
                                    c_re[1].astype(f32), c_im[1].astype(f32)), axis=1)
    y = (y_fwd + y_bwd).reshape(bsz, s, SSM_WIDTH) + d.astype(f32) * uf
    gl = jax.nn.gelu(y)
    y = gl * jax.nn.sigmoid(gl @ w_glu.astype(f32) + b_glu.astype(f32))
    y = y.astype(hn.dtype) * jax.nn.silu(z)
    return y @ w_out


def _trunk(x, attn_norm, attn_w_in, attn_w_out, ssm_norm, ssm_w_in, ssm_a_re, ssm_a_im, ssm_log_step,
           ssm_b_re, ssm_b_im, ssm_c_re, ssm_c_im, ssm_d, ssm_w_glu, ssm_b_glu, ssm_w_out, final_norm):
    for i in range(DEPTH):
        j = i // N_MIXERS
        if i % N_MIXERS == 0:
            x = x + _attention_mixer(_rmsnorm(x, attn_norm[j]), attn_w_in[j], attn_w_out[j])
        else:
            x = x + _ssm_mixer(_rmsnorm(x, ssm_norm[j]), ssm_w_in[j], ssm_a_re[j], ssm_a_im[j], ssm_log_step[j],
                               ssm_b_re[j], ssm_b_im[j], ssm_c_re[j], ssm_c_im[j], ssm_d[j],
                               ssm_w_glu[j], ssm_b_glu[j], ssm_w_out[j])
    return _rmsnorm(x, final_norm)


def setup_inputs(seed: int = 0) -> dict:
    key = jax.random.key(seed)
    ks = jax.random.split(key, 24)
    f32 = jnp.float32
    nrm = jax.random.normal
    na, ns = N_ATTN_LAYERS, N_SSM_LAYERS
    g, p, cg = SSM_GROUPS, SSM_STATE, SSM_GROUP
    x_prompt = nrm(ks[0], (BATCH, SEQ, D_MODEL), f32)
    x_sample = nrm(ks[1], (DEC_BATCH, DEC_SEQ, D_MODEL), f32)
    attn_norm = 1.0 + 0.02 * nrm(ks[2], (na, D_MODEL), f32)
    attn_w_in = nrm(ks[3], (na, D_MODEL, ATT_IN_COLS), f32) * D_MODEL ** -0.5
    attn_w_out = nrm(ks[4], (na, ATT_WIDTH, D_MODEL), f32) * ATT_WIDTH ** -0.5
    ssm_norm = 1.0 + 0.02 * nrm(ks[5], (ns, D_MODEL), f32)
    ssm_w_in = nrm(ks[6], (ns, D_MODEL, 2 * SSM_WIDTH), f32) * D_MODEL ** -0.5
    ssm_a_re = -0.5 + 0.01 * nrm(ks[7], (ns, 2, g, p), f32)
    ssm_a_im = jnp.broadcast_to(jnp.pi * jnp.arange(p, dtype=f32), (ns, 2, g, p)) + 0.0 * nrm(ks[8], (ns, 2, g, p), f32)
    ssm_log_step = jax.random.uniform(ks[9], (ns, 2, g), f32, minval=math.log(DT_MIN), maxval=math.log(DT_MAX))
    b_scale = (2.0 * cg) ** -0.5
    ssm_b_re = nrm(ks[10], (ns, 2, g, p, cg), f32) * b_scale
    ssm_b_im = nrm(ks[11], (ns, 2, g, p, cg), f32) * b_scale
    c_scale = (2.0 * p) ** -0.5
    ssm_c_re = nrm(ks[12], (ns, 2, g, cg, p), f32) * c_scale
    ssm_c_im = nrm(ks[13], (ns, 2, g, cg, p), f32) * c_scale
    ssm_d = nrm(ks[14], (ns, SSM_WIDTH), f32)
    ssm_w_glu = nrm(ks[15], (ns, SSM_WIDTH, SSM_WIDTH), f32) * SSM_WIDTH ** -0.5
    ssm_b_glu = 0.01 * nrm(ks[16], (ns, SSM_WIDTH), f32)
    ssm_w_out = nrm(ks[17], (ns, SSM_WIDTH, D_MODEL), f32) * SSM_WIDTH ** -0.5
    final_norm = 1.0 + 0.02 * nrm(ks[18], (D_MODEL,), f32)
    return {"x_prompt": x_prompt, "x_sample": x_sample,
            "attn_norm": attn_norm, "attn_w_in": attn_w_in, "attn_w_out": attn_w_out,
            "ssm_norm": ssm_norm, "ssm_w_in": ssm_w_in, "ssm_a_re": ssm_a_re, "ssm_a_im": ssm_a_im,
            "ssm_log_step": ssm_log_step, "ssm_b_re": ssm_b_re, "ssm_b_im": ssm_b_im,
            "ssm_c_re": ssm_c_re, "ssm_c_im": ssm_c_im, "ssm_d": ssm_d,
            "ssm_w_glu": ssm_w_glu, "ssm_b_glu": ssm_b_glu, "ssm_w_out": ssm_w_out,
            "final_norm": final_norm}


def reference(x_prompt, x_sample, attn_norm, attn_w_in, attn_w_out, ssm_norm, ssm_w_in, ssm_a_re, ssm_a_im,
              ssm_log_step, ssm_b_re, ssm_b_im, ssm_c_re, ssm_c_im, ssm_d, ssm_w_glu, ssm_b_glu, ssm_w_out,
              final_norm):
    y_prompt = _trunk(x_prompt, attn_norm, attn_w_in, attn_w_out, ssm_norm, ssm_w_in, ssm_a_re, ssm_a_im,
                      ssm_log_step, ssm_b_re, ssm_b_im, ssm_c_re, ssm_c_im, ssm_d, ssm_w_glu, ssm_b_glu,
                      ssm_w_out, final_norm)
    y_sample = _trunk(x_sample, attn_norm, attn_w_in, attn_w_out, ssm_norm, ssm_w_in, ssm_a_re, ssm_a_im,
                      ssm_log_step, ssm_b_re, ssm_b_im, ssm_c_re, ssm_c_im, ssm_d, ssm_w_glu, ssm_b_glu,
                      ssm_w_out, final_norm)
    return (y_prompt, y_sample)
```

```python
import functools
import math

import jax
import jax.numpy as jnp
from jax import lax
from jax.experimental import pallas as pl
from jax.experimental.pallas import tpu as pltpu

F32 = jnp.float32
BF16 = jnp.bfloat16

D_MODEL = 1024
DEPTH = 4
GROUP_PATTERNS = ((128, 1), (512, 4), (2048, 16))
N_HEADS = 16
HEAD_DIM = 64
N_HEAD_PAIRS = N_HEADS // 2
LANES = 128
ROPE_THETA = 10000.0
SSM_GROUP = 16
SSM_GROUPS = D_MODEL // SSM_GROUP
SSM_STATE = 64
CHUNK = 16
CHUNK_COLS = CHUNK * SSM_GROUP
NORM_EPS = 1e-6
SUBLANES = 8
VMEM_LIMIT = 52 * 1024 * 1024


def _sigmoid(x):
    return 1.0 / (1.0 + jnp.exp(-x))


def _silu(x):
    return x * _sigmoid(x)


def _gelu_tanh(x):
    c = math.sqrt(2.0 / math.pi)
    return x * (0.5 * (1.0 + jnp.tanh(c * (x + 0.044715 * (x * x * x)))))


def _rmsnorm_rows(xf, gain):
    return xf * lax.rsqrt(jnp.mean(xf * xf, axis=-1, keepdims=True) + NORM_EPS) * gain


def _row_tile(seq_len, cap):
    tile = min(cap, seq_len)
    assert seq_len % tile == 0
    return tile


def _attn_inproj_kernel(x_ref, g_ref, w_ref, cos_ref, sin_ref, o_ref, hn_ref):
    j = pl.program_id(1)

    @pl.when(j == 0)
    def _():
        hn_ref[...] = _rmsnorm_rows(x_ref[...], g_ref[...]).astype(BF16)

    acc = jnp.dot(hn_ref[...], w_ref[...], preferred_element_type=F32)
    is_rope = jnp.logical_and(j < 9, lax.rem(j, 3) != 2)

    @pl.when(is_rope)
    def _():
        c = cos_ref[...]
        s = sin_ref[...]
        for hp in range(N_HEAD_PAIRS):
            t = acc[:, hp * LANES:(hp + 1) * LANES]
            o_ref[hp] = (t * c + pltpu.roll(t, 64, axis=1) * s).astype(BF16)

    @pl.when(jnp.logical_not(is_rope))
    def _():
        for hp in range(N_HEAD_PAIRS):
            o_ref[hp] = acc[:, hp * LANES:(hp + 1) * LANES].astype(BF16)


def _attn_inproj(x2d, gain, w_p, cos_t, sin_t, seq_len):
    ntok = x2d.shape[0]
    tm = _row_tile(seq_len, 1024)
    nj = w_p.shape[1] // D_MODEL
    pos_blocks = seq_len // tm
    return pl.pallas_call(
        _attn_inproj_kernel,
        grid=(ntok // tm, nj),
        in_specs=[
            pl.BlockSpec((tm, D_MODEL), lambda i, j: (i, 0)),
            pl.BlockSpec((1, D_MODEL), lambda i, j: (0, 0)),
            pl.BlockSpec((D_MODEL, D_MODEL), lambda i, j: (0, j)),
            pl.BlockSpec((tm, LANES), lambda i, j: (i % pos_blocks, 0)),
            pl.BlockSpec((tm, LANES), lambda i, j: (i % pos_blocks, 0)),
        ],
        out_specs=pl.BlockSpec((None, N_HEAD_PAIRS, tm, LANES), lambda i, j: (j, 0, i, 0)),
        out_shape=jax.ShapeDtypeStruct((nj, N_HEAD_PAIRS, ntok, LANES), BF16),
        scratch_shapes=[pltpu.VMEM((tm, D_MODEL), BF16)],
        compiler_params=pltpu.CompilerParams(
            dimension_semantics=("parallel", "arbitrary"), vmem_limit_bytes=VMEM_LIMIT),
        name="attn_inproj",
    )(x2d, gain, w_p, cos_t, sin_t)


def _ssm_inproj_kernel(x_ref, g_ref, w_ref, o_ref, hn_ref):
    @pl.when(pl.program_id(1) == 0)
    def _():
        hn_ref[...] = _rmsnorm_rows(x_ref[...], g_ref[...]).astype(BF16)

    o_ref[...] = jnp.dot(hn_ref[...], w_ref[...], preferred_element_type=F32).astype(BF16)


def _ssm_inproj(x2d, gain, w_bf16, seq_len):
    ntok = x2d.shape[0]
    tm = _row_tile(seq_len, 1024)
    nj = w_bf16.shape[1] // D_MODEL
    return pl.pallas_call(
        _ssm_inproj_kernel,
        grid=(ntok // tm, nj),
        in_specs=[
            pl.BlockSpec((tm, D_MODEL), lambda i, j: (i, 0)),
            pl.BlockSpec((1, D_MODEL), lambda i, j: (0, 0)),
            pl.BlockSpec((D_MODEL, D_MODEL), lambda i, j: (0, j)),
        ],
        out_specs=pl.BlockSpec((None, tm, D_MODEL), lambda i, j: (j, i, 0)),
        out_shape=jax.ShapeDtypeStruct((nj, ntok, D_MODEL), BF16),
        scratch_shapes=[pltpu.VMEM((tm, D_MODEL), BF16)],
        compiler_params=pltpu.CompilerParams(
            dimension_semantics=("parallel", "arbitrary"), vmem_limit_bytes=VMEM_LIMIT),
        name="ssm_inproj",
    )(x2d, gain, w_bf16)


def _attn_kernel(q_ref, k_ref, v_ref, o_ref, lse_ref, *, length, half, classes, tq, tk):
    lane = lax.broadcasted_iota(jnp.int32, (1, LANES), 1)
    head_of_lane = lax.rem(lane, 64) // 32
    first_head_out = lane < HEAD_DIM
    row = lax.broadcasted_iota(jnp.int32, (tq, tk), 0)
    col = lax.broadcasted_iota(jnp.int32, (tq, tk), 1)

    def q_block(ib, carry):
        l0 = pl.multiple_of(ib * tq, tq)
        ks = pl.multiple_of(jnp.clip(l0 - half, 0, length - tk), half)
        valid = jnp.abs(row - col + (l0 - ks)) <= half
        bias = jnp.where(valid, 0.0, -jnp.inf).astype(F32)
        for r in range(classes):
            cols = slice(r * LANES, (r + 1) * LANES)
            q = q_ref[pl.ds(l0, tq), cols]
            k = k_ref[pl.ds(ks, tk), cols]
            v = v_ref[pl.ds(ks, tk), cols]
            outs, lses = [], []
            for h in range(2):
                qh = jnp.where(head_of_lane == h, q, jnp.zeros_like(q))
                s = lax.dot_general(qh, k, (((1,), (1,)), ((), ())), preferred_element_type=F32) + bias
                m = jnp.max(s, axis=-1, keepdims=True)
                p = jnp.exp(s - m)
                den = jnp.sum(p, axis=-1, keepdims=True)
                pv = jnp.dot(p.astype(BF16), v, preferred_element_type=F32)
                outs.append(pv / den)
                lses.append(m + jnp.log(den))
            o_ref[pl.ds(l0, tq), cols] = jnp.where(first_head_out, outs[0], outs[1])
            lse_ref[pl.ds(l0, tq), cols] = jnp.where(first_head_out, lses[0], lses[1])
        return carry

    lax.fori_loop(0, length // tq, q_block, 0)


def _attention(qkvz, group, batch, seq_len):
    window, dilation = GROUP_PATTERNS[group]
    half = window // (2 * dilation)
    length = seq_len // dilation
    ntok = batch * seq_len
    classes = min(dilation, 4)
    tq = min(2 * half, length)
    tk = min(4 * half, length)
    assert length % tq == 0 and half % 16 == 0
    view = qkvz.reshape(qkvz.shape[0], N_HEAD_PAIRS, batch, length, dilation * LANES)
    in_block = (None, None, None, length, classes * LANES)
    out_block = (None, None, length, classes * LANES)

    def in_map(which):
        return lambda b, hp, rc: (3 * group + which, hp, b, 0, rc)

    out_spec = pl.BlockSpec(out_block, lambda b, hp, rc: (hp, b, 0, rc))
    out_sds = jax.ShapeDtypeStruct((N_HEAD_PAIRS, batch, length, dilation * LANES), F32)
    o, lse = pl.pallas_call(
        functools.partial(_attn_kernel, length=length, half=half, classes=classes, tq=tq, tk=tk),
        grid=(batch, N_HEAD_PAIRS, dilation // classes),
        in_specs=[pl.BlockSpec(in_block, in_map(0)), pl.BlockSpec(in_block, in_map(1)),
                  pl.BlockSpec(in_block, in_map(2))],
        out_specs=[out_spec, out_spec],
        out_shape=[out_sds, out_sds],
        compiler_params=pltpu.CompilerParams(
            dimension_semantics=("parallel", "parallel", "parallel"), vmem_limit_bytes=VMEM_LIMIT),
        name=f"banded_attn_g{group}",
    )(view, view, view)
    return o.reshape(N_HEAD_PAIRS, ntok, LANES), lse.reshape(N_HEAD_PAIRS, ntok, LANES)


def _attn_out_kernel(o0, o1, o2, l0, l1, l2, z_ref, x_ref, w_ref, out_ref, y_ref):
    for hp in range(N_HEAD_PAIRS):
        a0, a1, a2 = l0[hp], l1[hp], l2[hp]
        m = jnp.maximum(jnp.maximum(a0, a1), a2)
        e0, e1, e2 = jnp.exp(a0 - m), jnp.exp(a1 - m), jnp.exp(a2 - m)
        o = (e0 * o0[hp] + e1 * o1[hp] + e2 * o2[hp]) / (e0 + e1 + e2)
        y_ref[:, hp * LANES:(hp + 1) * LANES] = (o * _silu(z_ref[hp].astype(F32))).astype(BF16)
    out_ref[...] = x_ref[...] + jnp.dot(y_ref[...], w_ref[...], preferred_element_type=F32)


def _attn_out(outs, lses, qkvz, x2d, w_out, seq_len):
    ntok = x2d.shape[0]
    tm = _row_tile(seq_len, 256)
    gate_block = qkvz.shape[0] - 1
    hp_spec = pl.BlockSpec((N_HEAD_PAIRS, tm, LANES), lambda i: (0, i, 0))
    return pl.pallas_call(
        _attn_out_kernel,
        grid=(ntok // tm,),
        in_specs=[hp_spec] * 6 + [
            pl.BlockSpec((None, N_HEAD_PAIRS, tm, LANES), lambda i: (gate_block, 0, i, 0)),
            pl.BlockSpec((tm, D_MODEL), lambda i: (i, 0)),
            pl.BlockSpec((D_MODEL, D_MODEL), lambda i: (0, 0)),
        ],
        out_specs=pl.BlockSpec((tm, D_MODEL), lambda i: (i, 0)),
        out_shape=jax.ShapeDtypeStruct((ntok, D_MODEL), F32),
        scratch_shapes=[pltpu.VMEM((tm, D_MODEL), BF16)],
        compiler_params=pltpu.CompilerParams(
            dimension_semantics=("parallel",), vmem_limit_bytes=VMEM_LIMIT),
        name="attn_merge_outproj",
    )(*outs, *lses, qkvz, x2d, w_out)


def _ssm_core_kernel(u_ref, m_ref, w_ref, v_ref, a_ref, y_ref, s_ref, hf_ref, hb_ref, *, n_chunks, rows_per_chunk):
    bp = rows_per_chunk
    u = u_ref[...]
    s_ref[...] = jnp.dot(u, w_ref[...], preferred_element_type=F32)
    a_re = jnp.broadcast_to(a_ref[0:1, :], (bp, LANES))
    a_im = jnp.broadcast_to(a_ref[1:2, :], (bp, LANES))
    is_fwd = lax.broadcasted_iota(jnp.int32, (bp, LANES), 1) < SSM_STATE

    def step(i, carry):
        h_re, h_im = carry
        rf = pl.multiple_of(i * bp, bp)
        rb = pl.multiple_of((n_chunks - 1 - i) * bp, bp)
        hf_ref[pl.ds(rf, bp), 0:LANES] = h_re
        hf_ref[pl.ds(rf, bp), LANES:2 * LANES] = h_im
        hb_ref[pl.ds(rb, bp), 0:LANES] = h_re
        hb_ref[pl.ds(rb, bp), LANES:2 * LANES] = h_im
        s_re = jnp.where(is_fwd, s_ref[pl.ds(rf, bp), 0:LANES], s_ref[pl.ds(rb, bp), 0:LANES])
        s_im = jnp.where(is_fwd, s_ref[pl.ds(rf, bp), LANES:2 * LANES], s_ref[pl.ds(rb, bp), LANES:2 * LANES])
        n_re = a_re * h_re - a_im * h_im + s_re
        n_im = a_re * h_im + a_im * h_re + s_im
        return n_re, n_im

    zero = jnp.zeros((bp, LANES), F32)
    lax.fori_loop(0, n_chunks, step, (zero, zero))
    is_fwd_wide = lax.rem(lax.broadcasted_iota(jnp.int32, (1, CHUNK_COLS), 1), LANES) < SSM_STATE
    h0 = jnp.where(is_fwd_wide, hf_ref[...], hb_ref[...]).astype(BF16)
    y_ref[...] = (jnp.dot(u, m_ref[...], preferred_element_type=F32)
                  + jnp.dot(h0, v_ref[...], preferred_element_type=F32))


def _ssm_core(u_groups, m_mat, w_mat, v_mat, a_coef, n_chunks, rows_per_chunk):
    rows = n_chunks * rows_per_chunk
    mat_spec = pl.BlockSpec((None, CHUNK_COLS, CHUNK_COLS), lambda g: (g, 0, 0))
    row_spec = pl.BlockSpec((None, rows, CHUNK_COLS), lambda g: (g, 0, 0))
    return pl.pallas_call(
        functools.partial(_ssm_core_kernel, n_chunks=n_chunks, rows_per_chunk=rows_per_chunk),
        grid=(SSM_GROUPS,),
        in_specs=[row_spec, mat_spec, mat_spec, mat_spec,
                  pl.BlockSpec((None, 2, LANES), lambda g: (g, 0, 0))],
        out_specs=row_spec,
        out_shape=jax.ShapeDtypeStruct((SSM_GROUPS, rows, CHUNK_COLS), F32),
        scratch_shapes=[pltpu.VMEM((rows, CHUNK_COLS), F32)] * 3,
        compiler_params=pltpu.CompilerParams(
            dimension_semantics=("parallel",), vmem_limit_bytes=VMEM_LIMIT),
        name="ssm_chunk_scan",
    )(u_groups, m_mat, w_mat, v_mat, a_coef)


def _ssm_post_kernel(y_ref, z_ref, x_ref, wg_ref, bg_ref, wo_ref, fg_ref, out_ref, *, final):
    gl = _gelu_tanh(y_ref[...])
    t = jnp.dot(gl.astype(BF16), wg_ref[...], preferred_element_type=F32) + bg_ref[...]
    y = gl * _sigmoid(t)
    y = y * _silu(z_ref[...].astype(F32))
    res = x_ref[...] + jnp.dot(y.astype(BF16), wo_ref[...], preferred_element_type=F32)
    if final:
        res = _rmsnorm_rows(res, fg_ref[...])
    out_ref[...] = res


def _ssm_post(y2d, uz, x2d, w_glu, b_glu, w_out, final_gain, seq_len, final):
    ntok = x2d.shape[0]
    tm = _row_tile(seq_len, 512)
    row_spec = pl.BlockSpec((tm, D_MODEL), lambda i: (i, 0))
    vec_spec = pl.BlockSpec((1, D_MODEL), lambda i: (0, 0))
    mat_spec = pl.BlockSpec((D_MODEL, D_MODEL), lambda i: (0, 0))
    return pl.pallas_call(
        functools.partial(_ssm_post_kernel, final=final),
        grid=(ntok // tm,),
        in_specs=[row_spec, pl.BlockSpec((None, tm, D_MODEL), lambda i: (1, i, 0)), row_spec,
                  mat_spec, vec_spec, mat_spec, vec_spec],
        out_specs=row_spec,
        out_shape=jax.ShapeDtypeStruct((ntok, D_MODEL), F32),
        compiler_params=pltpu.CompilerParams(
            dimension_semantics=("parallel",), vmem_limit_bytes=VMEM_LIMIT),
        name="ssm_glu_outproj",
    )(y2d, uz, x2d, w_glu, b_glu, w_out, final_gain)


def _qk_column_permutation():
    i = jnp.arange(LANES)
    head_in_pair = (i % 64) // 32
    dim = (i // 64) * 32 + i % 32
    hp = jnp.arange(N_HEAD_PAIRS)[:, None]
    return ((2 * hp + head_in_pair[None, :]) * HEAD_DIM + dim[None, :]).reshape(-1)


def _prep_attn_w_in(w_in):
    perm = _qk_column_permutation()
    width = N_HEADS * HEAD_DIM
    blocks = []
    for g in range(len(GROUP_PATTERNS)):
        base = 3 * g * width
        q = w_in[:, base:base + width][:, perm] * (HEAD_DIM ** -0.5)
        k = w_in[:, base + width:base + 2 * width][:, perm]
        v = w_in[:, base + 2 * width:base + 3 * width]
        blocks += [q, k, v]
    blocks.append(w_in[:, 3 * len(GROUP_PATTERNS) * width:])
    return jnp.concatenate(blocks, axis=1).astype(BF16)


def _rope_tables(seq_len):
    inv_freq = ROPE_THETA ** (-jnp.arange(0, HEAD_DIM, 2, dtype=F32) / HEAD_DIM)
    ang = jnp.arange(seq_len, dtype=F32)[:, None] * inv_freq[None, :]
    cos, sin = jnp.cos(ang), jnp.sin(ang)
    cos_t = jnp.concatenate([cos] * 4, axis=1)
    sin_t = jnp.concatenate([-sin, -sin, sin, sin], axis=1)
    return cos_t, sin_t


def _prep_ssm(a_re, a_im, log_step, b_re, b_im, c_re, c_im, d):
    hi = lax.Precision.HIGHEST
    g_n, p_n, c_n, t_n = SSM_GROUPS, SSM_STATE, SSM_GROUP, CHUNK
    dt = jnp.exp(log_step)[..., None]
    ldr = a_re * dt
    ldi = a_im * dt
    abar_re = jnp.exp(ldr) * jnp.cos(ldi)
    abar_im = jnp.exp(ldr) * jnp.sin(ldi)
    den = a_re * a_re + a_im * a_im
    nr = abar_re - 1.0
    ni = abar_im
    f_re = (nr * a_re + ni * a_im) / den
    f_im = (ni * a_re - nr * a_im) / den
    bb_re = f_re[..., None] * b_re - f_im[..., None] * b_im
    bb_im = f_re[..., None] * b_im + f_im[..., None] * b_re
    n = jnp.arange(t_n + 1, dtype=F32)[:, None, None, None]
    pw_re = jnp.exp(n * ldr) * jnp.cos(n * ldi)
    pw_im = jnp.exp(n * ldr) * jnp.sin(n * ldi)
    ca_re = c_re[None] * pw_re[:, :, :, None, :] - c_im[None] * pw_im[:, :, :, None, :]
    ca_im = c_re[None] * pw_im[:, :, :, None, :] + c_im[None] * pw_re[:, :, :, None, :]
    kern = (jnp.einsum('ldgcp,dgpk->ldgck', ca_re, bb_re, precision=hi)
            - jnp.einsum('ldgcp,dgpk->ldgck', ca_im, bb_im, precision=hi))
    s_idx = jnp.arange(t_n)[:, None]
    t_idx = jnp.arange(t_n)[None, :]
    lag = t_idx - s_idx
    m_f = kern[jnp.clip(lag, 0, t_n - 1), 0] * (lag >= 0)[:, :, None, None, None].astype(F32)
    m_b = kern[jnp.clip(-lag, 0, t_n - 1), 1] * (lag <= 0)[:, :, None, None, None].astype(F32)
    m_mat = (m_f + m_b).transpose(2, 0, 4, 1, 3).reshape(g_n, CHUNK_COLS, CHUNK_COLS)
    skip = jnp.tile(d.reshape(g_n, c_n), (1, t_n))
    m_mat = m_mat + jnp.eye(CHUNK_COLS, dtype=F32)[None] * skip[:, None, :]

    def state_in(pw_r, pw_i, direction):
        re = pw_r[..., None] * bb_re[direction][None] - pw_i[..., None] * bb_im[direction][None]
        im = pw_r[..., None] * bb_im[direction][None] + pw_i[..., None] * bb_re[direction][None]
        to_cols = lambda x: x.transpose(1, 0, 3, 2).reshape(g_n, CHUNK_COLS, p_n)
        return to_cols(re), to_cols(im)

    rev = t_n - 1 - jnp.arange(t_n)
    wf_re, wf_im = state_in(pw_re[rev, 0], pw_im[rev, 0], 0)
    wb_re, wb_im = state_in(pw_re[:t_n, 1], pw_im[:t_n, 1], 1)
    w_mat = jnp.concatenate([wf_re, wb_re, wf_im, wb_im], axis=2)

    def state_out(ca_r, ca_i):
        to_rows = lambda x: x.transpose(1, 3, 0, 2).reshape(g_n, p_n, CHUNK_COLS)
        return to_rows(ca_r), to_rows(-ca_i)

    vf_re, vf_im = state_out(ca_re[1:t_n + 1, 0], ca_im[1:t_n + 1, 0])
    back = t_n - jnp.arange(t_n)
    vb_re, vb_im = state_out(ca_re[back, 1], ca_im[back, 1])
    v_mat = jnp.concatenate([vf_re, vb_re, vf_im, vb_im], axis=1)
    a_coef = jnp.stack([jnp.concatenate([pw_re[t_n, 0], pw_re[t_n, 1]], axis=-1),
                        jnp.concatenate([pw_im[t_n, 0], pw_im[t_n, 1]], axis=-1)], axis=1)
    return m_mat.astype(BF16), w_mat.astype(BF16), v_mat.astype(BF16), a_coef


def _attention_layer(x2d, batch, seq_len, params, rope):
    gain, w_in_p, w_out = params
    qkvz = _attn_inproj(x2d, gain, w_in_p, rope[0], rope[1], seq_len)
    outs, lses = [], []
    for g in range(len(GROUP_PATTERNS)):
        o, lse = _attention(qkvz, g, batch, seq_len)
        outs.append(o)
        lses.append(lse)
    return _attn_out(outs, lses, qkvz, x2d, w_out, seq_len)


def _ssm_layer(x2d, batch, seq_len, params, final_gain, final):
    gain, w_in, (m_mat, w_mat, v_mat, a_coef), w_glu, b_glu, w_out = params
    uz = _ssm_inproj(x2d, gain, w_in, seq_len)
    n_chunks = seq_len // CHUNK
    bp = -(-batch // SUBLANES) * SUBLANES
    u = uz[0].reshape(batch, n_chunks, CHUNK, SSM_GROUPS, SSM_GROUP).transpose(3, 1, 0, 2, 4)
    if bp != batch:
        u = jnp.pad(u, ((0, 0), (0, 0), (0, bp - batch), (0, 0), (0, 0)))
    u = u.reshape(SSM_GROUPS, n_chunks * bp, CHUNK_COLS)
    y = _ssm_core(u, m_mat, w_mat, v_mat, a_coef, n_chunks, bp)
    y = y.reshape(SSM_GROUPS, n_chunks, bp, CHUNK, SSM_GROUP)[:, :, :batch]
    y2d = y.transpose(2, 1, 3, 0, 4).reshape(batch * seq_len, D_MODEL)
    return _ssm_post(y2d, uz, x2d, w_glu, b_glu, w_out, final_gain, seq_len, final)


def _trunk(x, attn_params, ssm_params, final_gain):
    batch, seq_len, _ = x.shape
    rope = _rope_tables(seq_len)
    x2d = x.reshape(batch * seq_len, D_MODEL)
    for i in range(DEPTH):
        j = i // 2
        if i % 2 == 0:
            x2d = _attention_layer(x2d, batch, seq_len, attn_params[j], rope)
        else:
            x2d = _ssm_layer(x2d, batch, seq_len, ssm_params[j], final_gain, final=(i == DEPTH - 1))
    return x2d.reshape(batch, seq_len, D_MODEL)


def kernel(x_prompt, x_sample, attn_norm, attn_w_in, attn_w_out, ssm_norm, ssm_w_in, ssm_a_re, ssm_a_im, ssm_log_step, ssm_b_re, ssm_b_im, ssm_c_re, ssm_c_im, ssm_d, ssm_w_glu, ssm_b_glu, ssm_w_out, final_norm):
    attn_params = []
    for j in range(attn_norm.shape[0]):
        attn_params.append((attn_norm[j][None, :], _prep_attn_w_in(attn_w_in[j]), attn_w_out[j].astype(BF16)))
    ssm_params = []
    for j in range(ssm_norm.shape[0]):
        mats = _prep_ssm(ssm_a_re[j], ssm_a_im[j], ssm_log_step[j], ssm_b_re[j], ssm_b_im[j],
                         ssm_c_re[j], ssm_c_im[j], ssm_d[j])
        ssm_params.append((ssm_norm[j][None, :], ssm_w_in[j].astype(BF16), mats,
                           ssm_w_glu[j].astype(BF16), ssm_b_glu[j][None, :], ssm_w_out[j].astype(BF16)))
    final_gain = final_norm[None, :]
    y_prompt = _trunk(x_prompt, attn_params, ssm_params, final_gain)
    y_sample = _trunk(x_sample, attn_params, ssm_params, final_gain)
    return (y_prompt, y_sample)
```

```python
import functools
import math

import jax
import jax.numpy as jnp
from jax import lax
from jax.experimental import pallas as pl
from jax.experimental.pallas import tpu as pltpu

F32 = jnp.float32
BF16 = jnp.bfloat16

D_MODEL = 1024
DEPTH = 4
GROUP_PATTERNS = ((128, 1), (512, 4), (2048, 16))
N_HEADS = 16
HEAD_DIM = 64
N_HEAD_PAIRS = N_HEADS // 2
LANES = 128
LANE_BLOCKS = D_MODEL // LANES
ROPE_THETA = 10000.0
SSM_GROUP = 16
SSM_GROUPS = D_MODEL // SSM_GROUP
SSM_STATE = 64
CHUNK = 16
CHUNK_COLS = CHUNK * SSM_GROUP
TILE = CHUNK * LANES
TQ = 128
NORM_EPS = 1e-6
VMEM_LIMIT = 56 * 1024 * 1024

ATTN_SLABS = {"q": (0, 4, 7), "k": (1, 5, 8), "v": (2, 6, 9), "z": 3}
N_ATTN_SLABS = 10


def _sigmoid(x):
    return 1.0 / (1.0 + jnp.exp(-x))


def _silu(x):
    return x * _sigmoid(x)


def _gelu_tanh(x):
    c = math.sqrt(2.0 / math.pi)
    return x * (0.5 * (1.0 + jnp.tanh(c * (x + 0.044715 * (x * x * x)))))


def _rmsnorm_rows(xf, gain):
    return xf * lax.rsqrt(jnp.mean(xf * xf, axis=-1, keepdims=True) + NORM_EPS) * gain


def _cmul(ar, ai, br, bi):
    return ar * br - ai * bi, ar * bi + ai * br


def _params(*semantics):
    return pltpu.CompilerParams(dimension_semantics=semantics, vmem_limit_bytes=VMEM_LIMIT)


def _stage_lane_blocks(x, slab_ref):
    for c in range(LANE_BLOCKS):
        slab_ref[c] = x[:, c * LANES:(c + 1) * LANES]


def _strided_rows(slab_ref, start, count, stride):
    return jnp.concatenate(
        [slab_ref[c, pl.ds(start, count, stride=stride), :] for c in range(LANE_BLOCKS)], axis=1)


def _attn_inproj_kernel(x_ref, g_ref, w_ref, cos_ref, sin_ref, o_ref, hn_ref, xs_ref):
    j = pl.program_id(1)
    gain = g_ref[...]

    @pl.when(j == 0)
    def _():
        x = x_ref[...]
        _stage_lane_blocks(x, xs_ref)
        hn_ref[...] = _rmsnorm_rows(x, gain).astype(BF16)

    for g in (1, 2):
        @pl.when(j == ATTN_SLABS["q"][g])
        def _(dilation=GROUP_PATTERNS[g][1]):
            rpc = TILE // dilation
            for r in range(dilation):
                rows = _strided_rows(xs_ref, r, rpc, dilation)
                hn_ref[r * rpc:(r + 1) * rpc, :] = _rmsnorm_rows(rows, gain).astype(BF16)

    is_plain = functools.reduce(jnp.logical_or, [j == b for b in ATTN_SLABS["v"] + (ATTN_SLABS["z"],)])
    half = TILE // 2
    for part in range(2):
        rows = slice(part * half, (part + 1) * half)
        acc = jnp.dot(hn_ref[rows, :], w_ref[...], preferred_element_type=F32)

        @pl.when(jnp.logical_not(is_plain))
        def _(acc=acc, rows=rows):
            c = cos_ref[rows, :]
            s = sin_ref[rows, :]
            for hp in range(N_HEAD_PAIRS):
                t = acc[:, hp * LANES:(hp + 1) * LANES]
                o_ref[hp, rows, :] = (t * c + pltpu.roll(t, 64, axis=1) * s).astype(BF16)

        @pl.when(is_plain)
        def _(acc=acc, rows=rows):
            for hp in range(N_HEAD_PAIRS):
                o_ref[hp, rows, :] = acc[:, hp * LANES:(hp + 1) * LANES].astype(BF16)


def _attn_inproj(x2d, gain, w_p, cos_t, sin_t, seq_len):
    ntok = x2d.shape[0]
    assert seq_len % TILE == 0
    pos_blocks = seq_len // TILE

    def table_map(i, j):
        order = jnp.where(j >= ATTN_SLABS["q"][2], 2, jnp.where(j >= ATTN_SLABS["q"][1], 1, 0))
        return (order, i % pos_blocks, 0)

    return pl.pallas_call(
        _attn_inproj_kernel,
        grid=(ntok // TILE, N_ATTN_SLABS),
        in_specs=[
            pl.BlockSpec((TILE, D_MODEL), lambda i, j: (i, 0), pipeline_mode=pl.Buffered(1)),
            pl.BlockSpec((1, D_MODEL), lambda i, j: (0, 0)),
            pl.BlockSpec((D_MODEL, D_MODEL), lambda i, j: (0, j)),
            pl.BlockSpec((None, TILE, LANES), table_map),
            pl.BlockSpec((None, TILE, LANES), table_map),
        ],
        out_specs=pl.BlockSpec((None, N_HEAD_PAIRS, TILE, LANES), lambda i, j: (j, 0, i, 0)),
        out_shape=jax.ShapeDtypeStruct((N_ATTN_SLABS, N_HEAD_PAIRS, ntok, LANES), BF16),
        scratch_shapes=[pltpu.VMEM((TILE, D_MODEL), BF16), pltpu.VMEM((LANE_BLOCKS, TILE, LANES), F32)],
        compiler_params=_params("parallel", "arbitrary"),
        name="attn_inproj",
    )(x2d, gain, w_p, cos_t, sin_t)


def _fused_attn_kernel(q0, k0, v0, z_ref, q1, k1, v1, q2, k2, v2, y_ref, acc_ref, m_ref, l_ref, bias_ref,
                       *, seq_len):
    lane = lax.broadcasted_iota(jnp.int32, (1, LANES), 1)
    head_of_lane = lax.rem(lane, 64) // 32
    first_head_out = lane < HEAD_DIM
    groups = ((q0, k0, v0), (q1, k1, v1), (q2, k2, v2))

    tk_max = bias_ref.shape[2]
    row_i = lax.broadcasted_iota(jnp.int32, (TQ, tk_max), 0)
    col_i = lax.broadcasted_iota(jnp.int32, (TQ, tk_max), 1)
    for o in range(bias_ref.shape[0]):
        bias_ref[o] = jnp.where(jnp.abs(row_i - col_i + o * 64) <= 64, 0.0, -jnp.inf).astype(F32)

    def window(w, carry):
        base = pl.multiple_of(w * TILE, TILE)
        for g, (q_ref, k_ref, v_ref) in enumerate(groups):
            win, dilation = GROUP_PATTERNS[g]
            half = win // (2 * dilation)
            length = seq_len // dilation
            rpc = TILE // dilation
            nqb = rpc // TQ
            tk = min(4 * half, length)
            n_pieces = tk // half

            def block(idx, c2, g=g, q_ref=q_ref, k_ref=k_ref, v_ref=v_ref, dilation=dilation, half=half,
                      length=length, rpc=rpc, nqb=nqb, tk=tk, n_pieces=n_pieces):
                r = idx // nqb
                lw = (idx % nqb) * TQ
                l0 = w * rpc + lw
                ks = jnp.clip(l0 - half, 0, length - tk)
                bias = bias_ref[(l0 - ks) // half][:, :tk]
                q = q_ref[pl.ds(pl.multiple_of(base + r * rpc + lw, TQ), TQ), :]
                k_parts, v_parts = [], []
                for p in range(n_pieces):
                    l = ks + p * half
                    row = pl.multiple_of((l // rpc) * TILE + r * rpc + l % rpc, half)
                    k_parts.append(k_ref[pl.ds(row, half), :])
                    v_parts.append(v_ref[pl.ds(row, half), :])
                k = jnp.concatenate(k_parts, axis=0)
                v = jnp.concatenate(v_parts, axis=0)
                zero = jnp.zeros_like(q)
                q_heads = jnp.concatenate([jnp.where(head_of_lane == 0, q, zero),
                                           jnp.where(head_of_lane == 1, q, zero)], axis=0)
                s = lax.dot_general(q_heads, k, (((1,), (1,)), ((), ())), preferred_element_type=F32)
                s = s + jnp.concatenate([bias, bias], axis=0)
                m2 = jnp.max(s, axis=-1, keepdims=True)
                p2 = jnp.exp(s - m2)
                den2 = jnp.sum(p2, axis=-1, keepdims=True)
                pv2 = jnp.dot(p2.astype(BF16), v, preferred_element_type=F32)
                pv = jnp.where(first_head_out, pv2[:TQ], pv2[TQ:])
                mm = jnp.where(first_head_out, m2[:TQ], m2[TQ:])
                dd = jnp.where(first_head_out, den2[:TQ], den2[TQ:])
                if dilation == 1:
                    rows = pl.ds(pl.multiple_of(lw, TQ), TQ)
                else:
                    rows = pl.ds(lw * dilation + r, TQ, stride=dilation)
                if g == 0:
                    acc_ref[rows, :] = pv
                    m_ref[rows, :] = mm
                    l_ref[rows, :] = dd
                else:
                    m_old = m_ref[rows, :]
                    m_new = jnp.maximum(m_old, mm)
                    e_old = jnp.exp(m_old - m_new)
                    e_new = jnp.exp(mm - m_new)
                    acc_ref[rows, :] = acc_ref[rows, :] * e_old + pv * e_new
                    l_ref[rows, :] = l_ref[rows, :] * e_old + dd * e_new
                    if g + 1 < len(groups):
                        m_ref[rows, :] = m_new
                return c2

            lax.fori_loop(0, dilation * nqb, block, 0, unroll=2)
        gate = _silu(z_ref[pl.ds(base, TILE), :].astype(F32))
        y_ref[pl.ds(base, TILE), :] = (acc_ref[...] / l_ref[...] * gate).astype(BF16)
        return carry

    lax.fori_loop(0, seq_len // TILE, window, 0)


def _fused_attention(qkvz, batch, seq_len):
    view = qkvz.reshape(N_ATTN_SLABS, N_HEAD_PAIRS, batch, seq_len, LANES)
    slab_bytes = seq_len * LANES * 2
    mode = {} if 2 * N_ATTN_SLABS * slab_bytes <= VMEM_LIMIT // 2 else {"pipeline_mode": pl.Buffered(1)}

    def slab_spec(slab):
        return pl.BlockSpec((None, None, None, seq_len, LANES), lambda b, hp: (slab, hp, b, 0, 0), **mode)

    order = [ATTN_SLABS["q"][0], ATTN_SLABS["k"][0], ATTN_SLABS["v"][0], ATTN_SLABS["z"],
             ATTN_SLABS["q"][1], ATTN_SLABS["k"][1], ATTN_SLABS["v"][1],
             ATTN_SLABS["q"][2], ATTN_SLABS["k"][2], ATTN_SLABS["v"][2]]
    tk_max = min(4 * 64, seq_len // GROUP_PATTERNS[0][1])
    y = pl.pallas_call(
        functools.partial(_fused_attn_kernel, seq_len=seq_len),
        grid=(batch, N_HEAD_PAIRS),
        in_specs=[slab_spec(s) for s in order],
        out_specs=pl.BlockSpec((None, None, seq_len, LANES), lambda b, hp: (hp, b, 0, 0)),
        out_shape=jax.ShapeDtypeStruct((N_HEAD_PAIRS, batch, seq_len, LANES), BF16),
        scratch_shapes=[pltpu.VMEM((TILE, LANES), F32)] * 3 + [pltpu.VMEM((3, TQ, tk_max), F32)],
        compiler_params=_params("parallel", "parallel"),
        name="fused_banded_attn",
    )(*([view] * N_ATTN_SLABS))
    return y.reshape(N_HEAD_PAIRS, batch * seq_len, LANES)


def _attn_out_kernel(y_ref, x_ref, w_ref, out_ref):
    y = jnp.concatenate([y_ref[hp] for hp in range(N_HEAD_PAIRS)], axis=1)
    out_ref[...] = x_ref[...] + jnp.dot(y, w_ref[...], preferred_element_type=F32)


def _attn_out(y, x2d, w_out):
    ntok = x2d.shape[0]
    tm = 512
    return pl.pallas_call(
        _attn_out_kernel,
        grid=(ntok // tm,),
        in_specs=[
            pl.BlockSpec((N_HEAD_PAIRS, tm, LANES), lambda i: (0, i, 0)),
            pl.BlockSpec((tm, D_MODEL), lambda i: (i, 0)),
            pl.BlockSpec((D_MODEL, D_MODEL), lambda i: (0, 0)),
        ],
        out_specs=pl.BlockSpec((tm, D_MODEL), lambda i: (i, 0)),
        out_shape=jax.ShapeDtypeStruct((ntok, D_MODEL), F32),
        compiler_params=_params("parallel"),
        name="attn_outproj",
    )(y, x2d, w_out)


def _ssm_inproj_kernel(x_ref, g_ref, wt_ref, o_ref, hn_ref, xs_ref):
    @pl.when(pl.program_id(1) == 0)
    def _():
        _stage_lane_blocks(x_ref[...], xs_ref)
        gain = g_ref[...]
        for t in range(CHUNK):
            rows = _strided_rows(xs_ref, t, LANES, CHUNK)
            hn_ref[t * LANES:(t + 1) * LANES, :] = _rmsnorm_rows(rows, gain).astype(BF16)

    half = TILE // 2
    for part in range(2):
        cols = slice(part * half, (part + 1) * half)
        o_ref[:, cols] = lax.dot_general(wt_ref[...], hn_ref[cols, :], (((1,), (1,)), ((), ())),
                                         preferred_element_type=F32).astype(BF16)


def _ssm_inproj(x2d, gain, w_in_t):
    ntok = x2d.shape[0]
    return pl.pallas_call(
        _ssm_inproj_kernel,
        grid=(ntok // TILE, 2),
        in_specs=[
            pl.BlockSpec((TILE, D_MODEL), lambda i, j: (i, 0), pipeline_mode=pl.Buffered(1)),
            pl.BlockSpec((1, D_MODEL), lambda i, j: (0, 0)),
            pl.BlockSpec((D_MODEL, D_MODEL), lambda i, j: (j, 0)),
        ],
        out_specs=pl.BlockSpec((None, D_MODEL, TILE), lambda i, j: (j, 0, i)),
        out_shape=jax.ShapeDtypeStruct((2, D_MODEL, ntok), BF16),
        scratch_shapes=[pltpu.VMEM((TILE, D_MODEL), BF16), pltpu.VMEM((LANE_BLOCKS, TILE, LANES), F32)],
        compiler_params=_params("parallel", "arbitrary"),
        name="ssm_inproj",
    )(x2d, gain, w_in_t)


def _ssm_core_kernel(u_ref, mt_ref, wt_ref, vt_ref, a16_ref, apl_ref, y_ref, ut_ref, s3_ref, h3_ref,
                     *, n_seq, blocks_per_seq):
    n_tiles = n_seq * blocks_per_seq
    p = SSM_STATE
    for tile in range(n_tiles):
        for t in range(CHUNK):
            src = (tile * CHUNK + t) * LANES
            ut_ref[t * SSM_GROUP:(t + 1) * SSM_GROUP, tile * LANES:(tile + 1) * LANES] = u_ref[:, src:src + LANES]
    ut = ut_ref[...]
    st = jnp.dot(wt_ref[...], ut, preferred_element_type=F32)
    for blk in range(n_tiles):
        s3_ref[blk] = st[:, blk * LANES:(blk + 1) * LANES]

    lane = lax.broadcasted_iota(jnp.int32, (1, LANES), 1)
    cur = (a16_ref[0:2 * p, :], a16_ref[2 * p:4 * p, :])
    levels = []
    shift = 1
    while shift < LANES:
        levels.append((shift, cur[0], cur[1]))
        cur = _cmul(cur[0], cur[1], cur[0], cur[1])
        shift *= 2
    apl_re = apl_ref[0:2 * p, :]
    apl_im = apl_ref[2 * p:4 * p, :]

    def scan_block(s_re, s_im, h_re, h_im, fwd):
        rows = slice(0, p) if fwd else slice(p, 2 * p)
        p_re, p_im = s_re, s_im
        for sh, lr, li in levels:
            if fwd:
                keep = lane >= sh
                amount = sh
            else:
                keep = lane < LANES - sh
                amount = LANES - sh
            sh_re = jnp.where(keep, pltpu.roll(p_re, amount, axis=1), 0.0)
            sh_im = jnp.where(keep, pltpu.roll(p_im, amount, axis=1), 0.0)
            d_re, d_im = _cmul(lr[rows], li[rows], sh_re, sh_im)
            p_re = p_re + d_re
            p_im = p_im + d_im
        c_re, c_im = _cmul(apl_re[rows], apl_im[rows], h_re, h_im)
        t_re = p_re + c_re
        t_im = p_im + c_im
        if fwd:
            first = lane >= 1
            h0_re = jnp.where(first, pltpu.roll(t_re, 1, axis=1), h_re)
            h0_im = jnp.where(first, pltpu.roll(t_im, 1, axis=1), h_im)
            return h0_re, h0_im, t_re[:, LANES - 1:LANES], t_im[:, LANES - 1:LANES]
        first = lane < LANES - 1
        h0_re = jnp.where(first, pltpu.roll(t_re, LANES - 1, axis=1), h_re)
        h0_im = jnp.where(first, pltpu.roll(t_im, LANES - 1, axis=1), h_im)
        return h0_re, h0_im, t_re[:, 0:1], t_im[:, 0:1]

    def seq_body(b, carry):
        zero = jnp.zeros((p, 1), F32)
        h_re, h_im = zero, zero
        for jb in range(blocks_per_seq):
            blk = b * blocks_per_seq + jb
            h0_re, h0_im, h_re, h_im = scan_block(s3_ref[blk, 0:p, :], s3_ref[blk, 2 * p:3 * p, :], h_re, h_im, True)
            h3_ref[blk, 0:p, :] = h0_re
            h3_ref[blk, 2 * p:3 * p, :] = h0_im
        h_re, h_im = zero, zero
        for jb in reversed(range(blocks_per_seq)):
            blk = b * blocks_per_seq + jb
            h0_re, h0_im, h_re, h_im = scan_block(s3_ref[blk, p:2 * p, :], s3_ref[blk, 3 * p:4 * p, :], h_re, h_im, False)
            h3_ref[blk, p:2 * p, :] = h0_re
            h3_ref[blk, 3 * p:4 * p, :] = h0_im
        return carry

    lax.fori_loop(0, n_seq, seq_body, 0)
    h0t = jnp.concatenate([h3_ref[blk] for blk in range(n_tiles)], axis=1).astype(BF16)
    yt = (jnp.dot(mt_ref[...], ut, preferred_element_type=F32)
          + jnp.dot(vt_ref[...], h0t, preferred_element_type=F32)).astype(BF16)
    for tile in range(n_tiles):
        for t in range(CHUNK):
            dst = (tile * CHUNK + t) * LANES
            y_ref[:, dst:dst + LANES] = yt[t * SSM_GROUP:(t + 1) * SSM_GROUP, tile * LANES:(tile + 1) * LANES]


def _ssm_core(uz_t, mats, batch, seq_len):
    m_t, w_t, v_t, a16, apl = mats
    ntok = batch * seq_len
    n_chunks = ntok // CHUNK
    blocks_per_seq = seq_len // TILE
    mat_spec = pl.BlockSpec((None, CHUNK_COLS, CHUNK_COLS), lambda g: (g, 0, 0))
    tab_spec = pl.BlockSpec((None, 4 * SSM_STATE, LANES), lambda g: (g, 0, 0))
    return pl.pallas_call(
        functools.partial(_ssm_core_kernel, n_seq=batch, blocks_per_seq=blocks_per_seq),
        grid=(SSM_GROUPS,),
        in_specs=[pl.BlockSpec((None, SSM_GROUP, ntok), lambda g: (0, g, 0)),
                  mat_spec, mat_spec, mat_spec, tab_spec, tab_spec],
        out_specs=pl.BlockSpec((SSM_GROUP, ntok), lambda g: (g, 0)),
        out_shape=jax.ShapeDtypeStruct((D_MODEL, ntok), BF16),
        scratch_shapes=[pltpu.VMEM((CHUNK_COLS, n_chunks), BF16),
                        pltpu.VMEM((n_chunks // LANES, 4 * SSM_STATE, LANES), F32),
                        pltpu.VMEM((n_chunks // LANES, 4 * SSM_STATE, LANES), F32)],
        compiler_params=_params("parallel"),
        name="ssm_chunk_scan",
    )(uz_t, m_t, w_t, v_t, a16, apl)


def _ssm_glu_kernel(y_ref, z_ref, wg_ref, bg_ref, o_ref):
    gl = _gelu_tanh(y_ref[...].astype(F32))
    t = jnp.dot(wg_ref[...], gl.astype(BF16), preferred_element_type=F32) + bg_ref[...]
    y = gl * _sigmoid(t)
    o_ref[...] = (y * _silu(z_ref[...].astype(F32))).astype(BF16)


def _ssm_glu(y_t, uz_t, w_glu_t, b_glu_col):
    ntok = y_t.shape[1]
    tl = 1024
    return pl.pallas_call(
        _ssm_glu_kernel,
        grid=(ntok // tl,),
        in_specs=[pl.BlockSpec((D_MODEL, tl), lambda i: (0, i)),
                  pl.BlockSpec((None, D_MODEL, tl), lambda i: (1, 0, i)),
                  pl.BlockSpec((D_MODEL, D_MODEL), lambda i: (0, 0)),
                  pl.BlockSpec((D_MODEL, 1), lambda i: (0, 0))],
        out_specs=pl.BlockSpec((D_MODEL, tl), lambda i: (0, i)),
        out_shape=jax.ShapeDtypeStruct((D_MODEL, ntok), BF16),
        compiler_params=_params("parallel"),
        name="ssm_glu",
    )(y_t, uz_t, w_glu_t, b_glu_col)


def _ssm_out_kernel(y_ref, x_ref, w_ref, fg_ref, out_ref, xs_ref, *, final):
    _stage_lane_blocks(x_ref[...], xs_ref)
    for t in range(CHUNK):
        res = lax.dot_general(y_ref[:, t * LANES:(t + 1) * LANES], w_ref[...], (((0,), (0,)), ((), ())),
                              preferred_element_type=F32)
        for c in range(LANE_BLOCKS):
            rows = pl.ds(t, LANES, stride=CHUNK)
            xs_ref[c, rows, :] = xs_ref[c, rows, :] + res[:, c * LANES:(c + 1) * LANES]
    out = jnp.concatenate([xs_ref[c] for c in range(LANE_BLOCKS)], axis=1)
    if final:
        out = _rmsnorm_rows(out, fg_ref[...])
    out_ref[...] = out


def _ssm_out(y3_t, x2d, w_out, final_gain, final):
    ntok = x2d.shape[0]
    return pl.pallas_call(
        functools.partial(_ssm_out_kernel, final=final),
        grid=(ntok // TILE,),
        in_specs=[pl.BlockSpec((D_MODEL, TILE), lambda i: (0, i)),
                  pl.BlockSpec((TILE, D_MODEL), lambda i: (i, 0), pipeline_mode=pl.Buffered(1)),
                  pl.BlockSpec((D_MODEL, D_MODEL), lambda i: (0, 0)),
                  pl.BlockSpec((1, D_MODEL), lambda i: (0, 0))],
        out_specs=pl.BlockSpec((TILE, D_MODEL), lambda i: (i, 0)),
        out_shape=jax.ShapeDtypeStruct((ntok, D_MODEL), F32),
        scratch_shapes=[pltpu.VMEM((LANE_BLOCKS, TILE, LANES), F32)],
        compiler_params=_params("parallel"),
        name="ssm_outproj",
    )(y3_t, x2d, w_out, final_gain)


def _qk_column_permutation():
    i = jnp.arange(LANES)
    head_in_pair = (i % 64) // 32
    dim = (i // 64) * 32 + i % 32
    hp = jnp.arange(N_HEAD_PAIRS)[:, None]
    return ((2 * hp + head_in_pair[None, :]) * HEAD_DIM + dim[None, :]).reshape(-1)


def _prep_attn_w_in(w_in):
    perm = _qk_column_permutation()
    width = N_HEADS * HEAD_DIM
    blocks = [None] * N_ATTN_SLABS
    for g in range(len(GROUP_PATTERNS)):
        base = 3 * g * width
        blocks[ATTN_SLABS["q"][g]] = w_in[:, base:base + width][:, perm] * (HEAD_DIM ** -0.5)
        blocks[ATTN_SLABS["k"][g]] = w_in[:, base + width:base + 2 * width][:, perm]
        blocks[ATTN_SLABS["v"][g]] = w_in[:, base + 2 * width:base + 3 * width]
    blocks[ATTN_SLABS["z"]] = w_in[:, 3 * len(GROUP_PATTERNS) * width:]
    return jnp.concatenate(blocks, axis=1).astype(BF16)


def _rope_tables(seq_len):
    inv_freq = ROPE_THETA ** (-jnp.arange(0, HEAD_DIM, 2, dtype=F32) / HEAD_DIM)
    ang = jnp.arange(seq_len, dtype=F32)[:, None] * inv_freq[None, :]
    cos, sin = jnp.cos(ang), jnp.sin(ang)
    cos_t = jnp.concatenate([cos] * 4, axis=1)
    sin_t = jnp.concatenate([-sin, -sin, sin, sin], axis=1)

    def orders(tab):
        out = []
        for _, dilation in GROUP_PATTERNS:
            rpc = TILE // dilation
            out.append(tab.reshape(seq_len // TILE, rpc, dilation, LANES).transpose(0, 2, 1, 3).reshape(seq_len, LANES))
        return jnp.stack(out)

    return orders(cos_t), orders(sin_t)


def _prep_ssm(a_re, a_im, log_step, b_re, b_im, c_re, c_im, d):
    hi = lax.Precision.HIGHEST
    g_n, p_n, c_n, t_n = SSM_GROUPS, SSM_STATE, SSM_GROUP, CHUNK
    dt = jnp.exp(log_step)[..., None]
    ldr = a_re * dt
    ldi = a_im * dt
    abar_re = jnp.exp(ldr) * jnp.cos(ldi)
    abar_im = jnp.exp(ldr) * jnp.sin(ldi)
    den = a_re * a_re + a_im * a_im
    nr = abar_re - 1.0
    ni = abar_im
    f_re = (nr * a_re + ni * a_im) / den
    f_im = (ni * a_re - nr * a_im) / den
    bb_re = f_re[..., None] * b_re - f_im[..., None] * b_im
    bb_im = f_re[..., None] * b_im + f_im[..., None] * b_re
    n = jnp.arange(t_n + 1, dtype=F32)[:, None, None, None]
    pw_re = jnp.exp(n * ldr) * jnp.cos(n * ldi)
    pw_im = jnp.exp(n * ldr) * jnp.sin(n * ldi)
    ca_re = c_re[None] * pw_re[:, :, :, None, :] - c_im[None] * pw_im[:, :, :, None, :]
    ca_im = c_re[None] * pw_im[:, :, :, None, :] + c_im[None] * pw_re[:, :, :, None, :]
    kern = (jnp.einsum('ldgcp,dgpk->ldgck', ca_re, bb_re, precision=hi)
            - jnp.einsum('ldgcp,dgpk->ldgck', ca_im, bb_im, precision=hi))
    s_idx = jnp.arange(t_n)[:, None]
    t_idx = jnp.arange(t_n)[None, :]
    lag = t_idx - s_idx
    m_f = kern[jnp.clip(lag, 0, t_n - 1), 0] * (lag >= 0)[:, :, None, None, None].astype(F32)
    m_b = kern[jnp.clip(-lag, 0, t_n - 1), 1] * (lag <= 0)[:, :, None, None, None].astype(F32)
    m_mat = (m_f + m_b).transpose(2, 0, 4, 1, 3).reshape(g_n, CHUNK_COLS, CHUNK_COLS)
    skip = jnp.tile(d.reshape(g_n, c_n), (1, t_n))
    m_mat = m_mat + jnp.eye(CHUNK_COLS, dtype=F32)[None] * skip[:, None, :]

    def state_in(pw_r, pw_i, direction):
        re = pw_r[..., None] * bb_re[direction][None] - pw_i[..., None] * bb_im[direction][None]
        im = pw_r[..., None] * bb_im[direction][None] + pw_i[..., None] * bb_re[direction][None]
        to_cols = lambda x: x.transpose(1, 0, 3, 2).reshape(g_n, CHUNK_COLS, p_n)
        return to_cols(re), to_cols(im)

    rev = t_n - 1 - jnp.arange(t_n)
    wf_re, wf_im = state_in(pw_re[rev, 0], pw_im[rev, 0], 0)
    wb_re, wb_im = state_in(pw_re[:t_n, 1], pw_im[:t_n, 1], 1)
    w_mat = jnp.concatenate([wf_re, wb_re, wf_im, wb_im], axis=2)

    def state_out(ca_r, ca_i):
        to_rows = lambda x: x.transpose(1, 3, 0, 2).reshape(g_n, p_n, CHUNK_COLS)
        return to_rows(ca_r), to_rows(-ca_i)

    vf_re, vf_im = state_out(ca_re[1:t_n + 1, 0], ca_im[1:t_n + 1, 0])
    back = t_n - jnp.arange(t_n)
    vb_re, vb_im = state_out(ca_re[back, 1], ca_im[back, 1])
    v_mat = jnp.concatenate([vf_re, vb_re, vf_im, vb_im], axis=1)

    def rows4(f_r, b_r, f_i, b_i):
        return jnp.concatenate([f_r, b_r, f_i, b_i], axis=1)

    ones = jnp.ones((1, 1, LANES), F32)
    a16 = rows4(pw_re[t_n, 0][..., None] * ones, pw_re[t_n, 1][..., None] * ones,
                pw_im[t_n, 0][..., None] * ones, pw_im[t_n, 1][..., None] * ones)
    k_lane = jnp.arange(LANES, dtype=F32)[None, None, :]
    n_f = t_n * (k_lane + 1.0)
    n_b = t_n * (LANES - k_lane)
    pow_lane = lambda nn, dr: (jnp.exp(nn * ldr[dr][..., None]) * jnp.cos(nn * ldi[dr][..., None]),
                               jnp.exp(nn * ldr[dr][..., None]) * jnp.sin(nn * ldi[dr][..., None]))
    plf_re, plf_im = pow_lane(n_f, 0)
    plb_re, plb_im = pow_lane(n_b, 1)
    apl = rows4(plf_re, plb_re, plf_im, plb_im)
    tr = lambda x: x.transpose(0, 2, 1).astype(BF16)
    return tr(m_mat), tr(w_mat), tr(v_mat), a16, apl


def _attention_layer(x2d, batch, seq_len, params, rope):
    gain, w_in_p, w_out = params
    qkvz = _attn_inproj(x2d, gain, w_in_p, rope[0], rope[1], seq_len)
    y = _fused_attention(qkvz, batch, seq_len)
    return _attn_out(y, x2d, w_out)


def _ssm_layer(x2d, batch, seq_len, params, final_gain, final):
    gain, w_in_t, mats, w_glu_t, b_glu_col, w_out = params
    uz_t = _ssm_inproj(x2d, gain, w_in_t)
    y_t = _ssm_core(uz_t, mats, batch, seq_len)
    y3_t = _ssm_glu(y_t, uz_t, w_glu_t, b_glu_col)
    return _ssm_out(y3_t, x2d, w_out, final_gain, final)


def _trunk(x, attn_params, ssm_params, final_gain):
    batch, seq_len, _ = x.shape
    assert seq_len % TILE == 0
    rope = _rope_tables(seq_len)
    x2d = x.reshape(batch * seq_len, D_MODEL)
    for i in range(DEPTH):
        j = i // 2
        if i % 2 == 0:
            x2d = _attention_layer(x2d, batch, seq_len, attn_params[j], rope)
        else:
            x2d = _ssm_layer(x2d, batch, seq_len, ssm_params[j], final_gain, final=(i == DEPTH - 1))
    return x2d.reshape(batch, seq_len, D_MODEL)


def _prepare(attn_norm, attn_w_in, attn_w_out, ssm_norm, ssm_w_in, ssm_a_re, ssm_a_im, ssm_log_step, ssm_b_re,
             ssm_b_im, ssm_c_re, ssm_c_im, ssm_d, ssm_w_glu, ssm_b_glu, ssm_w_out, final_norm):
    attn_params = []
    for j in range(attn_norm.shape[0]):
        attn_params.append((attn_norm[j][None, :], _prep_attn_w_in(attn_w_in[j]), attn_w_out[j].astype(BF16)))
    ssm_params = []
    for j in range(ssm_norm.shape[0]):
        mats = _prep_ssm(ssm_a_re[j], ssm_a_im[j], ssm_log_step[j], ssm_b_re[j], ssm_b_im[j],
                         ssm_c_re[j], ssm_c_im[j], ssm_d[j])
        ssm_params.append((ssm_norm[j][None, :], ssm_w_in[j].T.astype(BF16), mats,
                           ssm_w_glu[j].T.astype(BF16), ssm_b_glu[j][:, None], ssm_w_out[j].astype(BF16)))
    return attn_params, ssm_params, final_norm[None, :]


def kernel(x_prompt, x_sample, attn_norm, attn_w_in, attn_w_out, ssm_norm, ssm_w_in, ssm_a_re, ssm_a_im, ssm_log_step, ssm_b_re, ssm_b_im, ssm_c_re, ssm_c_im, ssm_d, ssm_w_glu, ssm_b_glu, ssm_w_out, final_norm):
    attn_params, ssm_params, final_gain = _prepare(
        attn_norm, attn_w_in, attn_w_out, ssm_norm, ssm_w_in, ssm_a_re, ssm_a_im, ssm_log_step, ssm_b_re,
        ssm_b_im, ssm_c_re, ssm_c_im, ssm_d, ssm_w_glu, ssm_b_glu, ssm_w_out, final_norm)
    y_prompt = _trunk(x_prompt, attn_params, ssm_params, final_gain)
    y_sample = _trunk(x_sample, attn_params, ssm_params, final_gain)
    return (y_prompt, y_sample)
```

```python
import functools
import math

import jax
import jax.numpy as jnp
from jax import lax
from jax.experimental import pallas as pl
from jax.experimental.pallas import tpu as pltpu

F32 = jnp.float32
BF16 = jnp.bfloat16

D_MODEL = 1024
DEPTH = 4
GROUP_PATTERNS = ((128, 1), (512, 4), (2048, 16))
N_HEADS = 16
HEAD_DIM = 64
N_HEAD_PAIRS = N_HEADS // 2
LANES = 128
LANE_BLOCKS = D_MODEL // LANES
ROPE_THETA = 10000.0
SSM_GROUP = 16
SSM_GROUPS = D_MODEL // SSM_GROUP
SSM_STATE = 64
CHUNK = 16
CHUNK_COLS = CHUNK * SSM_GROUP
TILE = CHUNK * LANES
TQ = 128
NORM_EPS = 1e-6
VMEM_LIMIT = 56 * 1024 * 1024

ATTN_SLABS = {"q": (0, 4, 7), "k": (1, 5, 8), "v": (2, 6, 9), "z": 3}
N_ATTN_SLABS = 10


def _sigmoid(x):
    return 1.0 / (1.0 + jnp.exp(-x))


def _silu(x):
    return x * _sigmoid(x)


def _gelu_tanh(x):
    c = math.sqrt(2.0 / math.pi)
    return x * (0.5 * (1.0 + jnp.tanh(c * (x + 0.044715 * (x * x * x)))))


def _rmsnorm_rows(xf, gain):
    return xf * lax.rsqrt(jnp.mean(xf * xf, axis=-1, keepdims=True) + NORM_EPS) * gain


def _cmul(ar, ai, br, bi):
    return ar * br - ai * bi, ar * bi + ai * br


def _params(*semantics):
    return pltpu.CompilerParams(dimension_semantics=semantics, vmem_limit_bytes=VMEM_LIMIT)


def _stage_lane_blocks(x, slab_ref):
    for c in range(LANE_BLOCKS):
        slab_ref[c] = x[:, c * LANES:(c + 1) * LANES]


def _strided_rows(slab_ref, start, count, stride):
    return jnp.concatenate(
        [slab_ref[c, pl.ds(start, count, stride=stride), :] for c in range(LANE_BLOCKS)], axis=1)


def _attn_inproj_kernel(x_ref, g_ref, w_ref, cos_ref, sin_ref, o_ref, hn_ref, xs_ref):
    j = pl.program_id(1)
    gain = g_ref[...]

    @pl.when(j == 0)
    def _():
        x = x_ref[...]
        _stage_lane_blocks(x, xs_ref)
        hn_ref[...] = _rmsnorm_rows(x, gain).astype(BF16)

    for g in (1, 2):
        @pl.when(j == ATTN_SLABS["q"][g])
        def _(dilation=GROUP_PATTERNS[g][1]):
            rpc = TILE // dilation
            for r in range(dilation):
                rows = _strided_rows(xs_ref, r, rpc, dilation)
                hn_ref[r * rpc:(r + 1) * rpc, :] = _rmsnorm_rows(rows, gain).astype(BF16)

    is_plain = functools.reduce(jnp.logical_or, [j == b for b in ATTN_SLABS["v"] + (ATTN_SLABS["z"],)])
    half = TILE // 2
    for part in range(2):
        rows = slice(part * half, (part + 1) * half)
        acc = jnp.dot(hn_ref[rows, :], w_ref[...], preferred_element_type=F32)

        @pl.when(jnp.logical_not(is_plain))
        def _(acc=acc, rows=rows):
            c = cos_ref[rows, :]
            s = sin_ref[rows, :]
            for hp in range(N_HEAD_PAIRS):
                t = acc[:, hp * LANES:(hp + 1) * LANES]
                o_ref[hp, rows, :] = (t * c + pltpu.roll(t, 64, axis=1) * s).astype(BF16)

        @pl.when(is_plain)
        def _(acc=acc, rows=rows):
            for hp in range(N_HEAD_PAIRS):
                o_ref[hp, rows, :] = acc[:, hp * LANES:(hp + 1) * LANES].astype(BF16)


def _attn_inproj(x2d, gain, w_p, cos_t, sin_t, seq_len):
    ntok = x2d.shape[0]
    assert seq_len % TILE == 0
    pos_blocks = seq_len // TILE

    def table_map(i, j):
        order = jnp.where(j >= ATTN_SLABS["q"][2], 2, jnp.where(j >= ATTN_SLABS["q"][1], 1, 0))
        return (order, i % pos_blocks, 0)

    return pl.pallas_call(
        _attn_inproj_kernel,
        grid=(ntok // TILE, N_ATTN_SLABS),
        in_specs=[
            pl.BlockSpec((TILE, D_MODEL), lambda i, j: (i, 0), pipeline_mode=pl.Buffered(1)),
            pl.BlockSpec((1, D_MODEL), lambda i, j: (0, 0)),
            pl.BlockSpec((D_MODEL, D_MODEL), lambda i, j: (0, j)),
            pl.BlockSpec((None, TILE, LANES), table_map),
            pl.BlockSpec((None, TILE, LANES), table_map),
        ],
        out_specs=pl.BlockSpec((None, N_HEAD_PAIRS, TILE, LANES), lambda i, j: (j, 0, i, 0)),
        out_shape=jax.ShapeDtypeStruct((N_ATTN_SLABS, N_HEAD_PAIRS, ntok, LANES), BF16),
        scratch_shapes=[pltpu.VMEM((TILE, D_MODEL), BF16), pltpu.VMEM((LANE_BLOCKS, TILE, LANES), F32)],
        compiler_params=_params("parallel", "arbitrary"),
        name="attn_inproj",
    )(x2d, gain, w_p, cos_t, sin_t)


def _fused_attn_kernel(q0, k0, v0, z_ref, q1, k1, v1, q2, k2, v2, y_ref, acc_ref, m_ref, l_ref, bias_ref,
                       s_all, p_all, m_all, *, seq_len):
    lane = lax.broadcasted_iota(jnp.int32, (1, LANES), 1)
    head_of_lane = lax.rem(lane, 64) // 32
    first_head_out = lane < HEAD_DIM
    groups = ((q0, k0, v0), (q1, k1, v1), (q2, k2, v2))

    tk_max = bias_ref.shape[2]
    row_i = lax.rem(lax.broadcasted_iota(jnp.int32, (2 * TQ, tk_max), 0), TQ)
    col_i = lax.broadcasted_iota(jnp.int32, (2 * TQ, tk_max), 1)
    for o in range(bias_ref.shape[0]):
        bias_ref[o] = jnp.where(jnp.abs(row_i - col_i + o * 64) <= 64, 0.0, -jnp.inf).astype(F32)

    def window(w, carry):
        base = pl.multiple_of(w * TILE, TILE)
        for g, (q_ref, k_ref, v_ref) in enumerate(groups):
            win, dilation = GROUP_PATTERNS[g]
            half = win // (2 * dilation)
            length = seq_len // dilation
            rpc = TILE // dilation
            nqb = rpc // TQ
            tk = min(4 * half, length)
            n_pieces = tk // half
            n_blocks = dilation * nqb

            def locate(idx, dilation=dilation, half=half, length=length, rpc=rpc, nqb=nqb, tk=tk):
                r = idx // nqb
                lw = (idx % nqb) * TQ
                l0 = w * rpc + lw
                ks = jnp.clip(l0 - half, 0, length - tk)
                return r, lw, l0, ks

            def key_rows(ref, r, ks, rpc=rpc, half=half, n_pieces=n_pieces):
                parts = []
                for p in range(n_pieces):
                    l = ks + p * half
                    row = pl.multiple_of((l // rpc) * TILE + r * rpc + l % rpc, half)
                    parts.append(ref[pl.ds(row, half), :])
                return jnp.concatenate(parts, axis=0)

            def scores(idx, c2, q_ref=q_ref, k_ref=k_ref, rpc=rpc, half=half, tk=tk):
                r, lw, l0, ks = locate(idx)
                q = q_ref[pl.ds(pl.multiple_of(base + r * rpc + lw, TQ), TQ), :]
                k = key_rows(k_ref, r, ks)
                zero = jnp.zeros_like(q)
                q_heads = jnp.concatenate([jnp.where(head_of_lane == 0, q, zero),
                                           jnp.where(head_of_lane == 1, q, zero)], axis=0)
                s = lax.dot_general(q_heads, k, (((1,), (1,)), ((), ())), preferred_element_type=F32)
                s_all[idx, :, 0:tk] = s + bias_ref[(l0 - ks) // half][:, 0:tk]
                return c2

            def softmax(idx, c2, tk=tk):
                s = s_all[idx, :, 0:tk]
                m2 = jnp.max(s, axis=-1, keepdims=True)
                p_all[idx, :, 0:tk] = jnp.exp(s - m2).astype(BF16)
                m_all[idx] = jnp.where(first_head_out, m2[:TQ], m2[TQ:])
                return c2

            def values(idx, c2, g=g, v_ref=v_ref, dilation=dilation, tk=tk):
                r, lw, l0, ks = locate(idx)
                v = key_rows(v_ref, r, ks)
                v_aug = jnp.concatenate([v, jnp.ones_like(v)], axis=1)
                pv2 = jnp.dot(p_all[idx, :, 0:tk], v_aug, preferred_element_type=F32)
                pv = jnp.where(first_head_out, pv2[:TQ, :LANES], pv2[TQ:, :LANES])
                dd = jnp.where(first_head_out, pv2[:TQ, LANES:], pv2[TQ:, LANES:])
                mm = m_all[idx]
                if dilation == 1:
                    rows = pl.ds(pl.multiple_of(lw, TQ), TQ)
                else:
                    rows = pl.ds(lw * dilation + r, TQ, stride=dilation)
                if g == 0:
                    acc_ref[rows, :] = pv
                    m_ref[rows, :] = mm
                    l_ref[rows, :] = dd
                else:
                    m_old = m_ref[rows, :]
                    m_new = jnp.maximum(m_old, mm)
                    e_old = jnp.exp(m_old - m_new)
                    e_new = jnp.exp(mm - m_new)
                    acc_ref[rows, :] = acc_ref[rows, :] * e_old + pv * e_new
                    l_ref[rows, :] = l_ref[rows, :] * e_old + dd * e_new
                    if g + 1 < len(groups):
                        m_ref[rows, :] = m_new
                return c2

            lax.fori_loop(0, n_blocks, scores, 0, unroll=8)
            lax.fori_loop(0, n_blocks, softmax, 0, unroll=4)
            lax.fori_loop(0, n_blocks, values, 0, unroll=4)
        gate = _silu(z_ref[pl.ds(base, TILE), :].astype(F32))
        y_ref[pl.ds(base, TILE), :] = (acc_ref[...] / l_ref[...] * gate).astype(BF16)
        return carry

    lax.fori_loop(0, seq_len // TILE, window, 0)


def _fused_attention(qkvz, batch, seq_len):
    view = qkvz.reshape(N_ATTN_SLABS, N_HEAD_PAIRS, batch, seq_len, LANES)
    slab_bytes = seq_len * LANES * 2
    mode = {} if 2 * N_ATTN_SLABS * slab_bytes <= VMEM_LIMIT // 2 else {"pipeline_mode": pl.Buffered(1)}

    def slab_spec(slab):
        return pl.BlockSpec((None, None, None, seq_len, LANES), lambda b, hp: (slab, hp, b, 0, 0), **mode)

    order = [ATTN_SLABS["q"][0], ATTN_SLABS["k"][0], ATTN_SLABS["v"][0], ATTN_SLABS["z"],
             ATTN_SLABS["q"][1], ATTN_SLABS["k"][1], ATTN_SLABS["v"][1],
             ATTN_SLABS["q"][2], ATTN_SLABS["k"][2], ATTN_SLABS["v"][2]]
    tk_max = min(4 * 64, seq_len // GROUP_PATTERNS[0][1])
    y = pl.pallas_call(
        functools.partial(_fused_attn_kernel, seq_len=seq_len),
        grid=(batch, N_HEAD_PAIRS),
        in_specs=[slab_spec(s) for s in order],
        out_specs=pl.BlockSpec((None, None, seq_len, LANES), lambda b, hp: (hp, b, 0, 0)),
        out_shape=jax.ShapeDtypeStruct((N_HEAD_PAIRS, batch, seq_len, LANES), BF16),
        scratch_shapes=[pltpu.VMEM((TILE, LANES), F32)] * 3 + [
            pltpu.VMEM((3, 2 * TQ, tk_max), F32),
            pltpu.VMEM((TILE // TQ, 2 * TQ, tk_max), F32),
            pltpu.VMEM((TILE // TQ, 2 * TQ, tk_max), BF16),
            pltpu.VMEM((TILE // TQ, TQ, LANES), F32)],
        compiler_params=_params("parallel", "parallel"),
        name="fused_banded_attn",
    )(*([view] * N_ATTN_SLABS))
    return y.reshape(N_HEAD_PAIRS, batch * seq_len, LANES)


def _attn_out_kernel(y_ref, x_ref, w_ref, out_ref):
    y = jnp.concatenate([y_ref[hp] for hp in range(N_HEAD_PAIRS)], axis=1)
    out_ref[...] = x_ref[...] + jnp.dot(y, w_ref[...], preferred_element_type=F32)


def _attn_out(y, x2d, w_out):
    ntok = x2d.shape[0]
    tm = 512
    return pl.pallas_call(
        _attn_out_kernel,
        grid=(ntok // tm,),
        in_specs=[
            pl.BlockSpec((N_HEAD_PAIRS, tm, LANES), lambda i: (0, i, 0)),
            pl.BlockSpec((tm, D_MODEL), lambda i: (i, 0)),
            pl.BlockSpec((D_MODEL, D_MODEL), lambda i: (0, 0)),
        ],
        out_specs=pl.BlockSpec((tm, D_MODEL), lambda i: (i, 0)),
        out_shape=jax.ShapeDtypeStruct((ntok, D_MODEL), F32),
        compiler_params=_params("parallel"),
        name="attn_outproj",
    )(y, x2d, w_out)


def _ssm_inproj_kernel(x_ref, g_ref, wt_ref, o_ref, hn_ref, xs_ref):
    @pl.when(pl.program_id(1) == 0)
    def _():
        _stage_lane_blocks(x_ref[...], xs_ref)
        gain = g_ref[...]
        for t in range(CHUNK):
            rows = _strided_rows(xs_ref, t, LANES, CHUNK)
            hn_ref[t * LANES:(t + 1) * LANES, :] = _rmsnorm_rows(rows, gain).astype(BF16)

    half = TILE // 2
    for part in range(2):
        cols = slice(part * half, (part + 1) * half)
        o_ref[:, cols] = lax.dot_general(wt_ref[...], hn_ref[cols, :], (((1,), (1,)), ((), ())),
                                         preferred_element_type=F32).astype(BF16)


def _ssm_inproj(x2d, gain, w_in_t):
    ntok = x2d.shape[0]
    return pl.pallas_call(
        _ssm_inproj_kernel,
        grid=(ntok // TILE, 2),
        in_specs=[
            pl.BlockSpec((TILE, D_MODEL), lambda i, j: (i, 0), pipeline_mode=pl.Buffered(1)),
            pl.BlockSpec((1, D_MODEL), lambda i, j: (0, 0)),
            pl.BlockSpec((D_MODEL, D_MODEL), lambda i, j: (j, 0)),
        ],
        out_specs=pl.BlockSpec((None, D_MODEL, TILE), lambda i, j: (j, 0, i)),
        out_shape=jax.ShapeDtypeStruct((2, D_MODEL, ntok), BF16),
        scratch_shapes=[pltpu.VMEM((TILE, D_MODEL), BF16), pltpu.VMEM((LANE_BLOCKS, TILE, LANES), F32)],
        compiler_params=_params("parallel", "arbitrary"),
        name="ssm_inproj",
    )(x2d, gain, w_in_t)


def _ssm_core_kernel(u_ref, mt_ref, wt_ref, vt_ref, a16_ref, apl_ref, y_ref, ut_ref, s3_ref, h3_ref,
                     *, n_seq, blocks_per_seq):
    n_tiles = n_seq * blocks_per_seq
    p = SSM_STATE
    for tile in range(n_tiles):
        for t in range(CHUNK):
            src = (tile * CHUNK + t) * LANES
            ut_ref[t * SSM_GROUP:(t + 1) * SSM_GROUP, tile * LANES:(tile + 1) * LANES] = u_ref[:, src:src + LANES]
    ut = ut_ref[...]
    st = jnp.dot(wt_ref[...], ut, preferred_element_type=F32)
    for blk in range(n_tiles):
        s3_ref[blk] = st[:, blk * LANES:(blk + 1) * LANES]

    lane = lax.broadcasted_iota(jnp.int32, (1, LANES), 1)
    cur = (a16_ref[0:2 * p, :], a16_ref[2 * p:4 * p, :])
    levels = []
    shift = 1
    while shift < LANES:
        levels.append((shift, cur[0], cur[1]))
        cur = _cmul(cur[0], cur[1], cur[0], cur[1])
        shift *= 2
    apl_re = apl_ref[0:2 * p, :]
    apl_im = apl_ref[2 * p:4 * p, :]

    def scan_block(s_re, s_im, h_re, h_im, fwd):
        rows = slice(0, p) if fwd else slice(p, 2 * p)
        p_re, p_im = s_re, s_im
        for sh, lr, li in levels:
            if fwd:
                keep = lane >= sh
                amount = sh
            else:
                keep = lane < LANES - sh
                amount = LANES - sh
            sh_re = jnp.where(keep, pltpu.roll(p_re, amount, axis=1), 0.0)
            sh_im = jnp.where(keep, pltpu.roll(p_im, amount, axis=1), 0.0)
            d_re, d_im = _cmul(lr[rows], li[rows], sh_re, sh_im)
            p_re = p_re + d_re
            p_im = p_im + d_im
        c_re, c_im = _cmul(apl_re[rows], apl_im[rows], h_re, h_im)
        t_re = p_re + c_re
        t_im = p_im + c_im
        if fwd:
            first = lane >= 1
            h0_re = jnp.where(first, pltpu.roll(t_re, 1, axis=1), h_re)
            h0_im = jnp.where(first, pltpu.roll(t_im, 1, axis=1), h_im)
            return h0_re, h0_im, t_re[:, LANES - 1:LANES], t_im[:, LANES - 1:LANES]
        first = lane < LANES - 1
        h0_re = jnp.where(first, pltpu.roll(t_re, LANES - 1, axis=1), h_re)
        h0_im = jnp.where(first, pltpu.roll(t_im, LANES - 1, axis=1), h_im)
        return h0_re, h0_im, t_re[:, 0:1], t_im[:, 0:1]

    def seq_body(b, carry):
        zero = jnp.zeros((p, 1), F32)
        h_re, h_im = zero, zero
        for jb in range(blocks_per_seq):
            blk = b * blocks_per_seq + jb
            h0_re, h0_im, h_re, h_im = scan_block(s3_ref[blk, 0:p, :], s3_ref[blk, 2 * p:3 * p, :], h_re, h_im, True)
            h3_ref[blk, 0:p, :] = h0_re
            h3_ref[blk, 2 * p:3 * p, :] = h0_im
        h_re, h_im = zero, zero
        for jb in reversed(range(blocks_per_seq)):
            blk = b * blocks_per_seq + jb
            h0_re, h0_im, h_re, h_im = scan_block(s3_ref[blk, p:2 * p, :], s3_ref[blk, 3 * p:4 * p, :], h_re, h_im, False)
            h3_ref[blk, p:2 * p, :] = h0_re
            h3_ref[blk, 3 * p:4 * p, :] = h0_im
        return carry

    lax.fori_loop(0, n_seq, seq_body, 0)
    h0t = jnp.concatenate([h3_ref[blk] for blk in range(n_tiles)], axis=1).astype(BF16)
    yt = (jnp.dot(mt_ref[...], ut, preferred_element_type=F32)
          + jnp.dot(vt_ref[...], h0t, preferred_element_type=F32)).astype(BF16)
    for tile in range(n_tiles):
        for t in range(CHUNK):
            dst = (tile * CHUNK + t) * LANES
            y_ref[:, dst:dst + LANES] = yt[t * SSM_GROUP:(t + 1) * SSM_GROUP, tile * LANES:(tile + 1) * LANES]


def _ssm_core(uz_t, mats, batch, seq_len):
    m_t, w_t, v_t, a16, apl = mats
    ntok = batch * seq_len
    n_chunks = ntok // CHUNK
    blocks_per_seq = seq_len // TILE
    mat_spec = pl.BlockSpec((None, CHUNK_COLS, CHUNK_COLS), lambda g: (g, 0, 0))
    tab_spec = pl.BlockSpec((None, 4 * SSM_STATE, LANES), lambda g: (g, 0, 0))
    return pl.pallas_call(
        functools.partial(_ssm_core_kernel, n_seq=batch, blocks_per_seq=blocks_per_seq),
        grid=(SSM_GROUPS,),
        in_specs=[pl.BlockSpec((None, SSM_GROUP, ntok), lambda g: (0, g, 0)),
                  mat_spec, mat_spec, mat_spec, tab_spec, tab_spec],
        out_specs=pl.BlockSpec((SSM_GROUP, ntok), lambda g: (g, 0)),
        out_shape=jax.ShapeDtypeStruct((D_MODEL, ntok), BF16),
        scratch_shapes=[pltpu.VMEM((CHUNK_COLS, n_chunks), BF16),
                        pltpu.VMEM((n_chunks // LANES, 4 * SSM_STATE, LANES), F32),
                        pltpu.VMEM((n_chunks // LANES, 4 * SSM_STATE, LANES), F32)],
        compiler_params=_params("parallel"),
        name="ssm_chunk_scan",
    )(uz_t, m_t, w_t, v_t, a16, apl)


def _ssm_glu_kernel(y_ref, z_ref, wg_ref, bg_ref, o_ref):
    gl = _gelu_tanh(y_ref[...].astype(F32))
    t = jnp.dot(wg_ref[...], gl.astype(BF16), preferred_element_type=F32) + bg_ref[...]
    y = gl * _sigmoid(t)
    o_ref[...] = (y * _silu(z_ref[...].astype(F32))).astype(BF16)


def _ssm_glu(y_t, uz_t, w_glu_t, b_glu_col):
    ntok = y_t.shape[1]
    tl = 1024
    return pl.pallas_call(
        _ssm_glu_kernel,
        grid=(ntok // tl,),
        in_specs=[pl.BlockSpec((D_MODEL, tl), lambda i: (0, i)),
                  pl.BlockSpec((None, D_MODEL, tl), lambda i: (1, 0, i)),
                  pl.BlockSpec((D_MODEL, D_MODEL), lambda i: (0, 0)),
                  pl.BlockSpec((D_MODEL, 1), lambda i: (0, 0))],
        out_specs=pl.BlockSpec((D_MODEL, tl), lambda i: (0, i)),
        out_shape=jax.ShapeDtypeStruct((D_MODEL, ntok), BF16),
        compiler_params=_params("parallel"),
        name="ssm_glu",
    )(y_t, uz_t, w_glu_t, b_glu_col)


def _ssm_out_kernel(y_ref, x_ref, w_ref, fg_ref, out_ref, xs_ref, *, final):
    _stage_lane_blocks(x_ref[...], xs_ref)
    half = TILE // 2
    for part in range(2):
        res = lax.dot_general(y_ref[:, part * half:(part + 1) * half], w_ref[...], (((0,), (0,)), ((), ())),
                              preferred_element_type=F32)
        for tt in range(CHUNK // 2):
            rows = pl.ds(part * (CHUNK // 2) + tt, LANES, stride=CHUNK)
            for c in range(LANE_BLOCKS):
                xs_ref[c, rows, :] = xs_ref[c, rows, :] + res[tt * LANES:(tt + 1) * LANES, c * LANES:(c + 1) * LANES]
    out = jnp.concatenate([xs_ref[c] for c in range(LANE_BLOCKS)], axis=1)
    if final:
        out = _rmsnorm_rows(out, fg_ref[...])
    out_ref[...] = out


def _ssm_out(y3_t, x2d, w_out, final_gain, final):
    ntok = x2d.shape[0]
    return pl.pallas_call(
        functools.partial(_ssm_out_kernel, final=final),
        grid=(ntok // TILE,),
        in_specs=[pl.BlockSpec((D_MODEL, TILE), lambda i: (0, i)),
                  pl.BlockSpec((TILE, D_MODEL), lambda i: (i, 0), pipeline_mode=pl.Buffered(1)),
                  pl.BlockSpec((D_MODEL, D_MODEL), lambda i: (0, 0)),
                  pl.BlockSpec((1, D_MODEL), lambda i: (0, 0))],
        out_specs=pl.BlockSpec((TILE, D_MODEL), lambda i: (i, 0)),
        out_shape=jax.ShapeDtypeStruct((ntok, D_MODEL), F32),
        scratch_shapes=[pltpu.VMEM((LANE_BLOCKS, TILE, LANES), F32)],
        compiler_params=_params("parallel"),
        name="ssm_outproj",
    )(y3_t, x2d, w_out, final_gain)


def _qk_column_permutation():
    i = jnp.arange(LANES)
    head_in_pair = (i % 64) // 32
    dim = (i // 64) * 32 + i % 32
    hp = jnp.arange(N_HEAD_PAIRS)[:, None]
    return ((2 * hp + head_in_pair[None, :]) * HEAD_DIM + dim[None, :]).reshape(-1)


def _prep_attn_w_in(w_in):
    perm = _qk_column_permutation()
    width = N_HEADS * HEAD_DIM
    blocks = [None] * N_ATTN_SLABS
    for g in range(len(GROUP_PATTERNS)):
        base = 3 * g * width
        blocks[ATTN_SLABS["q"][g]] = w_in[:, base:base + width][:, perm] * (HEAD_DIM ** -0.5)
        blocks[ATTN_SLABS["k"][g]] = w_in[:, base + width:base + 2 * width][:, perm]
        blocks[ATTN_SLABS["v"][g]] = w_in[:, base + 2 * width:base + 3 * width]
    blocks[ATTN_SLABS["z"]] = w_in[:, 3 * len(GROUP_PATTERNS) * width:]
    return jnp.concatenate(blocks, axis=1).astype(BF16)


def _rope_tables(seq_len):
    inv_freq = ROPE_THETA ** (-jnp.arange(0, HEAD_DIM, 2, dtype=F32) / HEAD_DIM)
    ang = jnp.arange(seq_len, dtype=F32)[:, None] * inv_freq[None, :]
    cos, sin = jnp.cos(ang), jnp.sin(ang)
    cos_t = jnp.concatenate([cos] * 4, axis=1)
    sin_t = jnp.concatenate([-sin, -sin, sin, sin], axis=1)

    def orders(tab):
        out = []
        for _, dilation in GROUP_PATTERNS:
            rpc = TILE // dilation
            out.append(tab.reshape(seq_len // TILE, rpc, dilation, LANES).transpose(0, 2, 1, 3).reshape(seq_len, LANES))
        return jnp.stack(out)

    return orders(cos_t), orders(sin_t)


def _prep_ssm(a_re, a_im, log_step, b_re, b_im, c_re, c_im, d):
    hi = lax.Precision.HIGHEST
    g_n, p_n, c_n, t_n = SSM_GROUPS, SSM_STATE, SSM_GROUP, CHUNK
    dt = jnp.exp(log_step)[..., None]
    ldr = a_re * dt
    ldi = a_im * dt
    abar_re = jnp.exp(ldr) * jnp.cos(ldi)
    abar_im = jnp.exp(ldr) * jnp.sin(ldi)
    den = a_re * a_re + a_im * a_im
    nr = abar_re - 1.0
    ni = abar_im
    f_re = (nr * a_re + ni * a_im) / den
    f_im = (ni * a_re - nr * a_im) / den
    bb_re = f_re[..., None] * b_re - f_im[..., None] * b_im
    bb_im = f_re[..., None] * b_im + f_im[..., None] * b_re
    n = jnp.arange(t_n + 1, dtype=F32)[:, None, None, None]
    pw_re = jnp.exp(n * ldr) * jnp.cos(n * ldi)
    pw_im = jnp.exp(n * ldr) * jnp.sin(n * ldi)
    ca_re = c_re[None] * pw_re[:, :, :, None, :] - c_im[None] * pw_im[:, :, :, None, :]
    ca_im = c_re[None] * pw_im[:, :, :, None, :] + c_im[None] * pw_re[:, :, :, None, :]
    kern = (jnp.einsum('ldgcp,dgpk->ldgck', ca_re, bb_re, precision=hi)
            - jnp.einsum('ldgcp,dgpk->ldgck', ca_im, bb_im, precision=hi))
    s_idx = jnp.arange(t_n)[:, None]
    t_idx = jnp.arange(t_n)[None, :]
    lag = t_idx - s_idx
    m_f = kern[jnp.clip(lag, 0, t_n - 1), 0] * (lag >= 0)[:, :, None, None, None].astype(F32)
    m_b = kern[jnp.clip(-lag, 0, t_n - 1), 1] * (lag <= 0)[:, :, None, None, None].astype(F32)
    m_mat = (m_f + m_b).transpose(2, 0, 4, 1, 3).reshape(g_n, CHUNK_COLS, CHUNK_COLS)
    skip = jnp.tile(d.reshape(g_n, c_n), (1, t_n))
    m_mat = m_mat + jnp.eye(CHUNK_COLS, dtype=F32)[None] * skip[:, None, :]

    def state_in(pw_r, pw_i, direction):
        re = pw_r[..., None] * bb_re[direction][None] - pw_i[..., None] * bb_im[direction][None]
        im = pw_r[..., None] * bb_im[direction][None] + pw_i[..., None] * bb_re[direction][None]
        to_cols = lambda x: x.transpose(1, 0, 3, 2).reshape(g_n, CHUNK_COLS, p_n)
        return to_cols(re), to_cols(im)

    rev = t_n - 1 - jnp.arange(t_n)
    wf_re, wf_im = state_in(pw_re[rev, 0], pw_im[rev, 0], 0)
    wb_re, wb_im = state_in(pw_re[:t_n, 1], pw_im[:t_n, 1], 1)
    w_mat = jnp.concatenate([wf_re, wb_re, wf_im, wb_im], axis=2)

    def state_out(ca_r, ca_i):
        to_rows = lambda x: x.transpose(1, 3, 0, 2).reshape(g_n, p_n, CHUNK_COLS)
        return to_rows(ca_r), to_rows(-ca_i)

    vf_re, vf_im = state_out(ca_re[1:t_n + 1, 0], ca_im[1:t_n + 1, 0])
    back = t_n - jnp.arange(t_n)
    vb_re, vb_im = state_out(ca_re[back, 1], ca_im[back, 1])
    v_mat = jnp.concatenate([vf_re, vb_re, vf_im, vb_im], axis=1)

    def rows4(f_r, b_r, f_i, b_i):
        return jnp.concatenate([f_r, b_r, f_i, b_i], axis=1)

    ones = jnp.ones((1, 1, LANES), F32)
    a16 = rows4(pw_re[t_n, 0][..., None] * ones, pw_re[t_n, 1][..., None] * ones,
                pw_im[t_n, 0][..., None] * ones, pw_im[t_n, 1][..., None] * ones)
    k_lane = jnp.arange(LANES, dtype=F32)[None, None, :]
    n_f = t_n * (k_lane + 1.0)
    n_b = t_n * (LANES - k_lane)
    pow_lane = lambda nn, dr: (jnp.exp(nn * ldr[dr][..., None]) * jnp.cos(nn * ldi[dr][..., None]),
                               jnp.exp(nn * ldr[dr][..., None]) * jnp.sin(nn * ldi[dr][..., None]))
    plf_re, plf_im = pow_lane(n_f, 0)
    plb_re, plb_im = pow_lane(n_b, 1)
    apl = rows4(plf_re, plb_re, plf_im, plb_im)
    tr = lambda x: x.transpose(0, 2, 1).astype(BF16)
    return tr(m_mat), tr(w_mat), tr(v_mat), a16, apl


def _attention_layer(x2d, batch, seq_len, params, rope):
    gain, w_in_p, w_out = params
    qkvz = _attn_inproj(x2d, gain, w_in_p, rope[0], rope[1], seq_len)
    y = _fused_attention(qkvz, batch, seq_len)
    return _attn_out(y, x2d, w_out)


def _ssm_layer(x2d, batch, seq_len, params, final_gain, final):
    gain, w_in_t, mats, w_glu_t, b_glu_col, w_out = params
    uz_t = _ssm_inproj(x2d, gain, w_in_t)
    y_t = _ssm_core(uz_t, mats, batch, seq_len)
    y3_t = _ssm_glu(y_t, uz_t, w_glu_t, b_glu_col)
    return _ssm_out(y3_t, x2d, w_out, final_gain, final)


def _trunk(x, attn_params, ssm_params, final_gain):
    batch, seq_len, _ = x.shape
    assert seq_len % TILE == 0
    rope = _rope_tables(seq_len)
    x2d = x.reshape(batch * seq_len, D_MODEL)
    for i in range(DEPTH):
        j = i // 2
        if i % 2 == 0:
            x2d = _attention_layer(x2d, batch, seq_len, attn_params[j], rope)
        else:
            x2d = _ssm_layer(x2d, batch, seq_len, ssm_params[j], final_gain, final=(i == DEPTH - 1))
    return x2d.reshape(batch, seq_len, D_MODEL)


def _prepare(attn_norm, attn_w_in, attn_w_out, ssm_norm, ssm_w_in, ssm_a_re, ssm_a_im, ssm_log_step, ssm_b_re,
             ssm_b_im, ssm_c_re, ssm_c_im, ssm_d, ssm_w_glu, ssm_b_glu, ssm_w_out, final_norm):
    attn_params = []
    for j in range(attn_norm.shape[0]):
        attn_params.append((attn_norm[j][None, :], _prep_attn_w_in(attn_w_in[j]), attn_w_out[j].astype(BF16)))
    ssm_params = []
    for j in range(ssm_norm.shape[0]):
        mats = _prep_ssm(ssm_a_re[j], ssm_a_im[j], ssm_log_step[j], ssm_b_re[j], ssm_b_im[j],
                         ssm_c_re[j], ssm_c_im[j], ssm_d[j])
        ssm_params.append((ssm_norm[j][None, :], ssm_w_in[j].T.astype(BF16), mats,
                           ssm_w_glu[j].T.astype(BF16), ssm_b_glu[j][:, None], ssm_w_out[j].astype(BF16)))
    return attn_params, ssm_params, final_norm[None, :]


def kernel(x_prompt, x_sample, attn_norm, attn_w_in, attn_w_out, ssm_norm, ssm_w_in, ssm_a_re, ssm_a_im, ssm_log_step, ssm_b_re, ssm_b_im, ssm_c_re, ssm_c_im, ssm_d, ssm_w_glu, ssm_b_glu, ssm_w_out, final_norm):
    attn_params, ssm_params, final_gain = _prepare(
        attn_norm, attn_w_in, attn_w_out, ssm_norm, ssm_w_in, ssm_a_re, ssm_a_im, ssm_log_step, ssm_b_re,
        ssm_b_im, ssm_c_re, ssm_c_im, ssm_d, ssm_w_glu, ssm_b_glu, ssm_w_out, final_norm)
    y_prompt = _trunk(x_prompt, attn_params, ssm_params, final_gain)
    y_sample = _trunk(x_sample, attn_params, ssm_params, final_gain)
    return (y_prompt, y_sample)
```

```python
import functools
import math

import jax
import jax.numpy as jnp
from jax import lax
from jax.experimental import pallas as pl
from jax.experimental.pallas import tpu as pltpu

F32 = jnp.float32
BF16 = jnp.bfloat16

D_MODEL = 1024
DEPTH = 4
GROUP_PATTERNS = ((128, 1), (512, 4), (2048, 16))
N_HEADS = 16
HEAD_DIM = 64
N_HEAD_PAIRS = N_HEADS // 2
LANES = 128
LANE_BLOCKS = D_MODEL // LANES
ROPE_THETA = 10000.0
SSM_GROUP = 16
SSM_GROUPS = D_MODEL // SSM_GROUP
SSM_STATE = 64
CHUNK = 16
CHUNK_COLS = CHUNK * SSM_GROUP
TILE = CHUNK * LANES
TQ = 128
NORM_EPS = 1e-6
VMEM_LIMIT = 56 * 1024 * 1024

ATTN_SLABS = {"q": (0, 4, 7), "k": (1, 5, 8), "v": (2, 6, 9), "z": 3}
N_ATTN_SLABS = 10


def _sigmoid(x):
    return 1.0 / (1.0 + jnp.exp(-x))


def _silu(x):
    return x * _sigmoid(x)


def _gelu_tanh(x):
    c = math.sqrt(2.0 / math.pi)
    return x * (0.5 * (1.0 + jnp.tanh(c * (x + 0.044715 * (x * x * x)))))


def _rmsnorm_rows(xf, gain):
    return xf * lax.rsqrt(jnp.mean(xf * xf, axis=-1, keepdims=True) + NORM_EPS) * gain


def _aligned(x, multiple):
    return x if isinstance(x, int) else pl.multiple_of(x, multiple)


def _clip(x, lo, hi):
    return min(max(x, lo), hi) if isinstance(x, int) else jnp.clip(x, lo, hi)


def _for_each(n, body, unroll):
    if unroll is None:
        for i in range(n):
            body(i)
    else:
        lax.fori_loop(0, n, lambda i, c: (body(i), c)[1], 0, unroll=unroll)


def _cmul(ar, ai, br, bi):
    return ar * br - ai * bi, ar * bi + ai * br


def _params(*semantics):
    return pltpu.CompilerParams(dimension_semantics=semantics, vmem_limit_bytes=VMEM_LIMIT)


def _stage_lane_blocks(x, slab_ref):
    for c in range(LANE_BLOCKS):
        slab_ref[c] = x[:, c * LANES:(c + 1) * LANES]


def _strided_rows(slab_ref, start, count, stride):
    return jnp.concatenate(
        [slab_ref[c, pl.ds(start, count, stride=stride), :] for c in range(LANE_BLOCKS)], axis=1)


def _attn_inproj_kernel(x_ref, g_ref, w_ref, cos_ref, sin_ref, o_ref, hn_ref, xs_ref):
    j = pl.program_id(1)
    gain = g_ref[...]

    @pl.when(j == 0)
    def _():
        x = x_ref[...]
        _stage_lane_blocks(x, xs_ref)
        hn_ref[...] = _rmsnorm_rows(x, gain).astype(BF16)

    for g in (1, 2):
        @pl.when(j == ATTN_SLABS["q"][g])
        def _(dilation=GROUP_PATTERNS[g][1]):
            rpc = TILE // dilation
            for r in range(dilation):
                rows = _strided_rows(xs_ref, r, rpc, dilation)
                hn_ref[r * rpc:(r + 1) * rpc, :] = _rmsnorm_rows(rows, gain).astype(BF16)

    half = TILE // 2
    for part in range(2):
        rows = slice(part * half, (part + 1) * half)
        acc = jnp.dot(hn_ref[rows, :], w_ref[...], preferred_element_type=F32)
        c = cos_ref[rows, :]
        s = sin_ref[rows, :]
        for hp in range(N_HEAD_PAIRS):
            t = acc[:, hp * LANES:(hp + 1) * LANES]
            o_ref[hp, rows, :] = (t * c + pltpu.roll(t, 64, axis=1) * s).astype(BF16)


def _attn_inproj(x2d, gain, w_p, cos_t, sin_t, seq_len):
    ntok = x2d.shape[0]
    assert seq_len % TILE == 0
    pos_blocks = seq_len // TILE

    def table_map(i, j):
        order = jnp.where(j >= ATTN_SLABS["q"][2], 2, jnp.where(j >= ATTN_SLABS["q"][1], 1, 0))
        is_plain = functools.reduce(jnp.logical_or, [j == b for b in ATTN_SLABS["v"] + (ATTN_SLABS["z"],)])
        return (jnp.where(is_plain, len(GROUP_PATTERNS), order), i % pos_blocks, 0)

    return pl.pallas_call(
        _attn_inproj_kernel,
        grid=(ntok // TILE, N_ATTN_SLABS),
        in_specs=[
            pl.BlockSpec((TILE, D_MODEL), lambda i, j: (i, 0)),
            pl.BlockSpec((1, D_MODEL), lambda i, j: (0, 0)),
            pl.BlockSpec((D_MODEL, D_MODEL), lambda i, j: (0, j)),
            pl.BlockSpec((None, TILE, LANES), table_map),
            pl.BlockSpec((None, TILE, LANES), table_map),
        ],
        out_specs=pl.BlockSpec((None, N_HEAD_PAIRS, TILE, LANES), lambda i, j: (j, 0, i, 0)),
        out_shape=jax.ShapeDtypeStruct((N_ATTN_SLABS, N_HEAD_PAIRS, ntok, LANES), BF16),
        scratch_shapes=[pltpu.VMEM((TILE, D_MODEL), BF16), pltpu.VMEM((LANE_BLOCKS, TILE, LANES), F32)],
        compiler_params=_params("parallel", "arbitrary"),
        name="attn_inproj",
    )(x2d, gain, w_p, cos_t, sin_t)


def _fused_attn_kernel(q0, k0, v0, z_ref, q1, k1, v1, q2, k2, v2, y_ref, acc_ref, m_ref, l_ref, bias_ref,
                       s_all, p_all, m_all, *, seq_len):
    lane = lax.broadcasted_iota(jnp.int32, (1, LANES), 1)
    head_of_lane = lax.rem(lane, 64) // 32
    first_head_out = lane < HEAD_DIM
    groups = ((q0, k0, v0), (q1, k1, v1), (q2, k2, v2))

    tk_max = bias_ref.shape[2]
    row_i = lax.rem(lax.broadcasted_iota(jnp.int32, (2 * TQ, tk_max), 0), TQ)
    col_i = lax.broadcasted_iota(jnp.int32, (2 * TQ, tk_max), 1)
    for o in range(bias_ref.shape[0]):
        bias_ref[o] = jnp.where(jnp.abs(row_i - col_i + o * 64) <= 64, 0.0, -jnp.inf).astype(F32)

    n_windows = seq_len // TILE
    static = n_windows == 1

    def window(w, carry):
        base = _aligned(w * TILE, TILE)
        for g, (q_ref, k_ref, v_ref) in enumerate(groups):
            win, dilation = GROUP_PATTERNS[g]
            half = win // (2 * dilation)
            length = seq_len // dilation
            rpc = TILE // dilation
            nqb = rpc // TQ
            tk = min(4 * half, length)
            n_pieces = tk // half
            n_blocks = dilation * nqb

            def locate(idx, dilation=dilation, half=half, length=length, rpc=rpc, nqb=nqb, tk=tk):
                r = idx // nqb
                lw = (idx % nqb) * TQ
                l0 = w * rpc + lw
                ks = _clip(l0 - half, 0, length - tk)
                return r, lw, l0, ks

            def key_rows(ref, r, ks, rpc=rpc, half=half, n_pieces=n_pieces):
                parts = []
                for p in range(n_pieces):
                    l = ks + p * half
                    row = _aligned((l // rpc) * TILE + r * rpc + l % rpc, half)
                    parts.append(ref[pl.ds(row, half), :])
                return jnp.concatenate(parts, axis=0)

            def scores(idx, q_ref=q_ref, k_ref=k_ref, rpc=rpc, half=half, tk=tk):
                r, lw, l0, ks = locate(idx)
                q = q_ref[pl.ds(_aligned(base + r * rpc + lw, TQ), TQ), :]
                k = key_rows(k_ref, r, ks)
                zero = jnp.zeros_like(q)
                q_heads = jnp.concatenate([jnp.where(head_of_lane == 0, q, zero),
                                           jnp.where(head_of_lane == 1, q, zero)], axis=0)
                s = lax.dot_general(q_heads, k, (((1,), (1,)), ((), ())), preferred_element_type=F32)
                s_all[idx, :, 0:tk] = s + bias_ref[(l0 - ks) // half][:, 0:tk]

            def softmax(idx, tk=tk):
                s = s_all[idx, :, 0:tk]
                m2 = jnp.max(s, axis=-1, keepdims=True)
                p_all[idx, :, 0:tk] = jnp.exp(s - m2).astype(BF16)
                m_all[idx] = jnp.where(first_head_out, m2[:TQ], m2[TQ:])

            def values(idx, g=g, v_ref=v_ref, dilation=dilation, tk=tk):
                r, lw, l0, ks = locate(idx)
                v = key_rows(v_ref, r, ks)
                v_aug = jnp.concatenate([v, jnp.ones_like(v)], axis=1)
                pv2 = jnp.dot(p_all[idx, :, 0:tk], v_aug, preferred_element_type=F32)
                pv = jnp.where(first_head_out, pv2[:TQ, :LANES], pv2[TQ:, :LANES])
                dd = jnp.where(first_head_out, pv2[:TQ, LANES:], pv2[TQ:, LANES:])
                mm = m_all[idx]
                if dilation == 1:
                    rows = pl.ds(_aligned(lw, TQ), TQ)
                else:
                    rows = pl.ds(lw * dilation + r, TQ, stride=dilation)
                if g == 0:
                    acc_ref[rows, :] = pv
                    m_ref[rows, :] = mm
                    l_ref[rows, :] = dd
                else:
                    m_old = m_ref[rows, :]
                    m_new = jnp.maximum(m_old, mm)
                    e_old = jnp.exp(m_old - m_new)
                    e_new = jnp.exp(mm - m_new)
                    acc_ref[rows, :] = acc_ref[rows, :] * e_old + pv * e_new
                    l_ref[rows, :] = l_ref[rows, :] * e_old + dd * e_new
                    if g + 1 < len(groups):
                        m_ref[rows, :] = m_new

            _for_each(n_blocks, scores, None if static else 8)
            _for_each(n_blocks, softmax, None if static else 4)
            _for_each(n_blocks, values, None if static else 4)
        gate = _silu(z_ref[pl.ds(base, TILE), :].astype(F32))
        y_ref[pl.ds(base, TILE), :] = (acc_ref[...] / l_ref[...] * gate).astype(BF16)
        return carry

    if static:
        window(0, 0)
    else:
        lax.fori_loop(0, n_windows, window, 0)


def _fused_attention(qkvz, batch, seq_len):
    view = qkvz.reshape(N_ATTN_SLABS, N_HEAD_PAIRS, batch, seq_len, LANES)
    slab_bytes = seq_len * LANES * 2
    mode = {} if 2 * N_ATTN_SLABS * slab_bytes <= VMEM_LIMIT // 2 else {"pipeline_mode": pl.Buffered(1)}

    def slab_spec(slab):
        return pl.BlockSpec((None, None, None, seq_len, LANES), lambda b, hp: (slab, hp, b, 0, 0), **mode)

    order = [ATTN_SLABS["q"][0], ATTN_SLABS["k"][0], ATTN_SLABS["v"][0], ATTN_SLABS["z"],
             ATTN_SLABS["q"][1], ATTN_SLABS["k"][1], ATTN_SLABS["v"][1],
             ATTN_SLABS["q"][2], ATTN_SLABS["k"][2], ATTN_SLABS["v"][2]]
    tk_max = min(4 * 64, seq_len // GROUP_PATTERNS[0][1])
    y = pl.pallas_call(
        functools.partial(_fused_attn_kernel, seq_len=seq_len),
        grid=(batch, N_HEAD_PAIRS),
        in_specs=[slab_spec(s) for s in order],
        out_specs=pl.BlockSpec((None, None, seq_len, LANES), lambda b, hp: (hp, b, 0, 0)),
        out_shape=jax.ShapeDtypeStruct((N_HEAD_PAIRS, batch, seq_len, LANES), BF16),
        scratch_shapes=[pltpu.VMEM((TILE, LANES), F32)] * 3 + [
            pltpu.VMEM((3, 2 * TQ, tk_max), F32),
            pltpu.VMEM((TILE // TQ, 2 * TQ, tk_max), F32),
            pltpu.VMEM((TILE // TQ, 2 * TQ, tk_max), BF16),
            pltpu.VMEM((TILE // TQ, TQ, LANES), F32)],
        compiler_params=_params("parallel", "parallel"),
        name="fused_banded_attn",
    )(*([view] * N_ATTN_SLABS))
    return y.reshape(N_HEAD_PAIRS, batch * seq_len, LANES)


def _attn_out_kernel(y_ref, x_ref, w_ref, out_ref):
    y = jnp.concatenate([y_ref[hp] for hp in range(N_HEAD_PAIRS)], axis=1)
    out_ref[...] = x_ref[...] + jnp.dot(y, w_ref[...], preferred_element_type=F32)


def _attn_out(y, x2d, w_out):
    ntok = x2d.shape[0]
    tm = 512
    return pl.pallas_call(
        _attn_out_kernel,
        grid=(ntok // tm,),
        in_specs=[
            pl.BlockSpec((N_HEAD_PAIRS, tm, LANES), lambda i: (0, i, 0)),
            pl.BlockSpec((tm, D_MODEL), lambda i: (i, 0)),
            pl.BlockSpec((D_MODEL, D_MODEL), lambda i: (0, 0)),
        ],
        out_specs=pl.BlockSpec((tm, D_MODEL), lambda i: (i, 0)),
        out_shape=jax.ShapeDtypeStruct((ntok, D_MODEL), F32),
        compiler_params=_params("parallel"),
        name="attn_outproj",
    )(y, x2d, w_out)


def _ssm_inproj_kernel(x_ref, g_ref, wt_ref, o_ref, hn_ref, xs_ref):
    @pl.when(pl.program_id(1) == 0)
    def _():
        _stage_lane_blocks(x_ref[...], xs_ref)
        gain = g_ref[...]
        for t in range(CHUNK):
            rows = _strided_rows(xs_ref, t, LANES, CHUNK)
            hn_ref[t * LANES:(t + 1) * LANES, :] = _rmsnorm_rows(rows, gain).astype(BF16)

    half = TILE // 2
    for part in range(2):
        cols = slice(part * half, (part + 1) * half)
        o_ref[:, cols] = lax.dot_general(wt_ref[...], hn_ref[cols, :], (((1,), (1,)), ((), ())),
                                         preferred_element_type=F32).astype(BF16)


def _ssm_inproj(x2d, gain, w_in_t):
    ntok = x2d.shape[0]
    return pl.pallas_call(
        _ssm_inproj_kernel,
        grid=(ntok // TILE, 2),
        in_specs=[
            pl.BlockSpec((TILE, D_MODEL), lambda i, j: (i, 0)),
            pl.BlockSpec((1, D_MODEL), lambda i, j: (0, 0)),
            pl.BlockSpec((D_MODEL, D_MODEL), lambda i, j: (j, 0)),
        ],
        out_specs=pl.BlockSpec((None, D_MODEL, TILE), lambda i, j: (j, 0, i)),
        out_shape=jax.ShapeDtypeStruct((2, D_MODEL, ntok), BF16),
        scratch_shapes=[pltpu.VMEM((TILE, D_MODEL), BF16), pltpu.VMEM((LANE_BLOCKS, TILE, LANES), F32)],
        compiler_params=_params("parallel", "arbitrary"),
        name="ssm_inproj",
    )(x2d, gain, w_in_t)


def _ssm_core_kernel(u_ref, mt_ref, w_ref, vt_ref, a16_ref, apl_ref, y_ref, ut_ref, s_ref, hf_ref, hb_ref,
                     *, n_seq, chunks_per_seq):
    n_tiles = n_seq * chunks_per_seq // LANES
    p = SSM_STATE
    sub = 8
    for tile in range(n_tiles):
        for t in range(CHUNK):
            src = (tile * CHUNK + t) * LANES
            ut_ref[t * SSM_GROUP:(t + 1) * SSM_GROUP, tile * LANES:(tile + 1) * LANES] = u_ref[:, src:src + LANES]
    ut = ut_ref[...]
    s_ref[...] = lax.dot_general(ut, w_ref[...], (((0,), (0,)), ((), ())), preferred_element_type=F32)

    lane = lax.broadcasted_iota(jnp.int32, (sub, LANES), 1)
    row = lax.broadcasted_iota(jnp.int32, (sub, LANES), 0)
    is_f = lane < p
    steps_in = jnp.where(is_f, row, sub - 1 - row)
    cur = (jnp.broadcast_to(a16_ref[0:1, :], (sub, LANES)), jnp.broadcast_to(a16_ref[1:2, :], (sub, LANES)))
    levels = []
    shift = 1
    while shift < sub:
        levels.append((shift, steps_in >= shift, cur))
        cur = _cmul(cur[0], cur[1], cur[0], cur[1])
        shift *= 2
    apl_re = apl_ref[0]
    apl_im = apl_ref[1]
    inner = steps_in >= 1
    n_steps = chunks_per_seq // sub

    def toward_scan_start(x, by):
        return jnp.where(is_f, pltpu.roll(x, by, axis=0), pltpu.roll(x, sub - by, axis=0))

    def seq_body(b, carry):
        seq0 = b * chunks_per_seq

        def step(i, h):
            h_re, h_im = h
            rf = pl.multiple_of(seq0 + i * sub, sub)
            rb = pl.multiple_of(seq0 + (n_steps - 1 - i) * sub, sub)
            x_re = jnp.where(is_f, s_ref[pl.ds(rf, sub), 0:LANES], s_ref[pl.ds(rb, sub), 0:LANES])
            x_im = jnp.where(is_f, s_ref[pl.ds(rf, sub), LANES:2 * LANES], s_ref[pl.ds(rb, sub), LANES:2 * LANES])
            for by, keep, (lr, li) in levels:
                d_re, d_im = _cmul(lr, li, jnp.where(keep, toward_scan_start(x_re, by), 0.0),
                                   jnp.where(keep, toward_scan_start(x_im, by), 0.0))
                x_re = x_re + d_re
                x_im = x_im + d_im
            c_re, c_im = _cmul(apl_re, apl_im, h_re, h_im)
            t_re = x_re + c_re
            t_im = x_im + c_im
            h0_re = jnp.where(inner, toward_scan_start(t_re, 1), h_re)
            h0_im = jnp.where(inner, toward_scan_start(t_im, 1), h_im)
            hf_ref[pl.ds(rf, sub), 0:LANES] = h0_re
            hf_ref[pl.ds(rf, sub), LANES:2 * LANES] = h0_im
            hb_ref[pl.ds(rb, sub), 0:LANES] = h0_re
            hb_ref[pl.ds(rb, sub), LANES:2 * LANES] = h0_im
            n_re = jnp.where(is_f[0:1], t_re[sub - 1:sub], t_re[0:1])
            n_im = jnp.where(is_f[0:1], t_im[sub - 1:sub], t_im[0:1])
            return n_re, n_im

        zero = jnp.zeros((1, LANES), F32)
        lax.fori_loop(0, n_steps, step, (zero, zero), unroll=4)
        return carry

    lax.fori_loop(0, n_seq, seq_body, 0)
    is_f_wide = lax.rem(lax.broadcasted_iota(jnp.int32, (1, 2 * LANES), 1), LANES) < p
    h0 = jnp.where(is_f_wide, hf_ref[...], hb_ref[...]).astype(BF16)
    yt = (jnp.dot(mt_ref[...], ut, preferred_element_type=F32)
          + lax.dot_general(vt_ref[...], h0, (((1,), (1,)), ((), ())), preferred_element_type=F32)).astype(BF16)
    for tile in range(n_tiles):
        for t in range(CHUNK):
            dst = (tile * CHUNK + t) * LANES
            y_ref[:, dst:dst + LANES] = yt[t * SSM_GROUP:(t + 1) * SSM_GROUP, tile * LANES:(tile + 1) * LANES]


def _ssm_core(uz_t, mats, batch, seq_len):
    m_t, w_n, v_t, a16, apl = mats
    ntok = batch * seq_len
    n_chunks = ntok // CHUNK
    mat_spec = pl.BlockSpec((None, CHUNK_COLS, CHUNK_COLS), lambda g: (g, 0, 0))
    return pl.pallas_call(
        functools.partial(_ssm_core_kernel, n_seq=batch, chunks_per_seq=seq_len // CHUNK),
        grid=(SSM_GROUPS,),
        in_specs=[pl.BlockSpec((None, SSM_GROUP, ntok), lambda g: (0, g, 0)),
                  mat_spec, mat_spec, mat_spec,
                  pl.BlockSpec((None, 2, LANES), lambda g: (g, 0, 0)),
                  pl.BlockSpec((None, 2, 8, LANES), lambda g: (g, 0, 0, 0))],
        out_specs=pl.BlockSpec((SSM_GROUP, ntok), lambda g: (g, 0)),
        out_shape=jax.ShapeDtypeStruct((D_MODEL, ntok), BF16),
        scratch_shapes=[pltpu.VMEM((CHUNK_COLS, n_chunks), BF16)] + [pltpu.VMEM((n_chunks, 2 * LANES), F32)] * 3,
        compiler_params=_params("parallel"),
        name="ssm_chunk_scan",
    )(uz_t, m_t, w_n, v_t, a16, apl)


def _ssm_glu_kernel(y_ref, z_ref, wg_ref, bg_ref, o_ref):
    gl = _gelu_tanh(y_ref[...].astype(F32))
    t = jnp.dot(wg_ref[...], gl.astype(BF16), preferred_element_type=F32) + bg_ref[...]
    y = gl * _sigmoid(t)
    o_ref[...] = (y * _silu(z_ref[...].astype(F32))).astype(BF16)


def _ssm_glu(y_t, uz_t, w_glu_t, b_glu_col):
    ntok = y_t.shape[1]
    tl = 1024
    return pl.pallas_call(
        _ssm_glu_kernel,
        grid=(ntok // tl,),
        in_specs=[pl.BlockSpec((D_MODEL, tl), lambda i: (0, i)),
                  pl.BlockSpec((None, D_MODEL, tl), lambda i: (1, 0, i)),
                  pl.BlockSpec((D_MODEL, D_MODEL), lambda i: (0, 0)),
                  pl.BlockSpec((D_MODEL, 1), lambda i: (0, 0))],
        out_specs=pl.BlockSpec((D_MODEL, tl), lambda i: (0, i)),
        out_shape=jax.ShapeDtypeStruct((D_MODEL, ntok), BF16),
        compiler_params=_params("parallel"),
        name="ssm_glu",
    )(y_t, uz_t, w_glu_t, b_glu_col)


def _ssm_out_kernel(y_ref, x_ref, w_ref, fg_ref, out_ref, xs_ref, *, final):
    _stage_lane_blocks(x_ref[...], xs_ref)
    half = TILE // 2
    for part in range(2):
        res = lax.dot_general(y_ref[:, part * half:(part + 1) * half], w_ref[...], (((0,), (0,)), ((), ())),
                              preferred_element_type=F32)
        for tt in range(CHUNK // 2):
            rows = pl.ds(part * (CHUNK // 2) + tt, LANES, stride=CHUNK)
            for c in range(LANE_BLOCKS):
                xs_ref[c, rows, :] = xs_ref[c, rows, :] + res[tt * LANES:(tt + 1) * LANES, c * LANES:(c + 1) * LANES]
    out = jnp.concatenate([xs_ref[c] for c in range(LANE_BLOCKS)], axis=1)
    if final:
        out = _rmsnorm_rows(out, fg_ref[...])
    out_ref[...] = out


def _ssm_out(y3_t, x2d, w_out, final_gain, final):
    ntok = x2d.shape[0]
    return pl.pallas_call(
        functools.partial(_ssm_out_kernel, final=final),
        grid=(ntok // TILE,),
        in_specs=[pl.BlockSpec((D_MODEL, TILE), lambda i: (0, i)),
                  pl.BlockSpec((TILE, D_MODEL), lambda i: (i, 0), pipeline_mode=pl.Buffered(1)),
                  pl.BlockSpec((D_MODEL, D_MODEL), lambda i: (0, 0)),
                  pl.BlockSpec((1, D_MODEL), lambda i: (0, 0))],
        out_specs=pl.BlockSpec((TILE, D_MODEL), lambda i: (i, 0)),
        out_shape=jax.ShapeDtypeStruct((ntok, D_MODEL), F32),
        scratch_shapes=[pltpu.VMEM((LANE_BLOCKS, TILE, LANES), F32)],
        compiler_params=_params("parallel"),
        name="ssm_outproj",
    )(y3_t, x2d, w_out, final_gain)


def _qk_column_permutation():
    i = jnp.arange(LANES)
    head_in_pair = (i % 64) // 32
    dim = (i // 64) * 32 + i % 32
    hp = jnp.arange(N_HEAD_PAIRS)[:, None]
    return ((2 * hp + head_in_pair[None, :]) * HEAD_DIM + dim[None, :]).reshape(-1)


def _prep_attn_w_in(w_in):
    perm = _qk_column_permutation()
    width = N_HEADS * HEAD_DIM
    blocks = [None] * N_ATTN_SLABS
    for g in range(len(GROUP_PATTERNS)):
        base = 3 * g * width
        blocks[ATTN_SLABS["q"][g]] = w_in[:, base:base + width][:, perm] * (HEAD_DIM ** -0.5)
        blocks[ATTN_SLABS["k"][g]] = w_in[:, base + width:base + 2 * width][:, perm]
        blocks[ATTN_SLABS["v"][g]] = w_in[:, base + 2 * width:base + 3 * width]
    blocks[ATTN_SLABS["z"]] = w_in[:, 3 * len(GROUP_PATTERNS) * width:]
    return jnp.concatenate(blocks, axis=1).astype(BF16)


def _rope_tables(seq_len):
    inv_freq = ROPE_THETA ** (-jnp.arange(0, HEAD_DIM, 2, dtype=F32) / HEAD_DIM)
    ang = jnp.arange(seq_len, dtype=F32)[:, None] * inv_freq[None, :]
    cos, sin = jnp.cos(ang), jnp.sin(ang)
    cos_t = jnp.concatenate([cos] * 4, axis=1)
    sin_t = jnp.concatenate([-sin, -sin, sin, sin], axis=1)

    def orders(tab, identity):
        out = []
        for _, dilation in GROUP_PATTERNS:
            rpc = TILE // dilation
            out.append(tab.reshape(seq_len // TILE, rpc, dilation, LANES).transpose(0, 2, 1, 3).reshape(seq_len, LANES))
        out.append(jnp.full_like(tab, identity))
        return jnp.stack(out)

    return orders(cos_t, 1.0), orders(sin_t, 0.0)


def _prep_ssm(a_re, a_im, log_step, b_re, b_im, c_re, c_im, d):
    hi = lax.Precision.HIGHEST
    g_n, p_n, c_n, t_n = SSM_GROUPS, SSM_STATE, SSM_GROUP, CHUNK
    dt = jnp.exp(log_step)[..., None]
    ldr = a_re * dt
    ldi = a_im * dt
    abar_re = jnp.exp(ldr) * jnp.cos(ldi)
    abar_im = jnp.exp(ldr) * jnp.sin(ldi)
    den = a_re * a_re + a_im * a_im
    nr = abar_re - 1.0
    ni = abar_im
    f_re = (nr * a_re + ni * a_im) / den
    f_im = (ni * a_re - nr * a_im) / den
    bb_re = f_re[..., None] * b_re - f_im[..., None] * b_im
    bb_im = f_re[..., None] * b_im + f_im[..., None] * b_re
    n = jnp.arange(t_n + 1, dtype=F32)[:, None, None, None]
    pw_re = jnp.exp(n * ldr) * jnp.cos(n * ldi)
    pw_im = jnp.exp(n * ldr) * jnp.sin(n * ldi)
    ca_re = c_re[None] * pw_re[:, :, :, None, :] - c_im[None] * pw_im[:, :, :, None, :]
    ca_im = c_re[None] * pw_im[:, :, :, None, :] + c_im[None] * pw_re[:, :, :, None, :]
    kern = (jnp.einsum('ldgcp,dgpk->ldgck', ca_re, bb_re, precision=hi)
            - jnp.einsum('ldgcp,dgpk->ldgck', ca_im, bb_im, precision=hi))
    s_idx = jnp.arange(t_n)[:, None]
    t_idx = jnp.arange(t_n)[None, :]
    lag = t_idx - s_idx
    m_f = kern[jnp.clip(lag, 0, t_n - 1), 0] * (lag >= 0)[:, :, None, None, None].astype(F32)
    m_b = kern[jnp.clip(-lag, 0, t_n - 1), 1] * (lag <= 0)[:, :, None, None, None].astype(F32)
    m_mat = (m_f + m_b).transpose(2, 0, 4, 1, 3).reshape(g_n, CHUNK_COLS, CHUNK_COLS)
    skip = jnp.tile(d.reshape(g_n, c_n), (1, t_n))
    m_mat = m_mat + jnp.eye(CHUNK_COLS, dtype=F32)[None] * skip[:, None, :]

    def state_in(pw_r, pw_i, direction):
        re = pw_r[..., None] * bb_re[direction][None] - pw_i[..., None] * bb_im[direction][None]
        im = pw_r[..., None] * bb_im[direction][None] + pw_i[..., None] * bb_re[direction][None]
        to_cols = lambda x: x.transpose(1, 0, 3, 2).reshape(g_n, CHUNK_COLS, p_n)
        return to_cols(re), to_cols(im)

    rev = t_n - 1 - jnp.arange(t_n)
    wf_re, wf_im = state_in(pw_re[rev, 0], pw_im[rev, 0], 0)
    wb_re, wb_im = state_in(pw_re[:t_n, 1], pw_im[:t_n, 1], 1)
    w_mat = jnp.concatenate([wf_re, wb_re, wf_im, wb_im], axis=2)

    def state_out(ca_r, ca_i):
        to_rows = lambda x: x.transpose(1, 3, 0, 2).reshape(g_n, p_n, CHUNK_COLS)
        return to_rows(ca_r), to_rows(-ca_i)

    vf_re, vf_im = state_out(ca_re[1:t_n + 1, 0], ca_im[1:t_n + 1, 0])
    back = t_n - jnp.arange(t_n)
    vb_re, vb_im = state_out(ca_re[back, 1], ca_im[back, 1])
    v_mat = jnp.concatenate([vf_re, vb_re, vf_im, vb_im], axis=1)

    lanes = lambda f, b: jnp.concatenate([f, b], axis=-1)
    a16 = jnp.stack([lanes(pw_re[t_n, 0], pw_re[t_n, 1]), lanes(pw_im[t_n, 0], pw_im[t_n, 1])], axis=1)
    j_row = jnp.arange(8, dtype=F32)[None, :, None]
    n_f = t_n * (j_row + 1.0)
    n_b = t_n * (8.0 - j_row)
    pow_rows = lambda nn, dr: (jnp.exp(nn * ldr[dr][:, None, :]) * jnp.cos(nn * ldi[dr][:, None, :]),
                               jnp.exp(nn * ldr[dr][:, None, :]) * jnp.sin(nn * ldi[dr][:, None, :]))
    plf_re, plf_im = pow_rows(n_f, 0)
    plb_re, plb_im = pow_rows(n_b, 1)
    apl = jnp.stack([lanes(plf_re, plb_re), lanes(plf_im, plb_im)], axis=1)
    tr = lambda x: x.transpose(0, 2, 1).astype(BF16)
    return tr(m_mat), w_mat.astype(BF16), tr(v_mat), a16, apl


def _attention_layer(x2d, batch, seq_len, params, rope):
    gain, w_in_p, w_out = params
    qkvz = _attn_inproj(x2d, gain, w_in_p, rope[0], rope[1], seq_len)
    y = _fused_attention(qkvz, batch, seq_len)
    return _attn_out(y, x2d, w_out)


def _ssm_layer(x2d, batch, seq_len, params, final_gain, final):
    gain, w_in_t, mats, w_glu_t, b_glu_col, w_out = params
    uz_t = _ssm_inproj(x2d, gain, w_in_t)
    y_t = _ssm_core(uz_t, mats, batch, seq_len)
    y3_t = _ssm_glu(y_t, uz_t, w_glu_t, b_glu_col)
    return _ssm_out(y3_t, x2d, w_out, final_gain, final)


def _trunk(x, attn_params, ssm_params, final_gain):
    batch, seq_len, _ = x.shape
    assert seq_len % TILE == 0
    rope = _rope_tables(seq_len)
    x2d = x.reshape(batch * seq_len, D_MODEL)
    for i in range(DEPTH):
        j = i // 2
        if i % 2 == 0:
            x2d = _attention_layer(x2d, batch, seq_len, attn_params[j], rope)
        else:
            x2d = _ssm_layer(x2d, batch, seq_len, ssm_params[j], final_gain, final=(i == DEPTH - 1))
    return x2d.reshape(batch, seq_len, D_MODEL)


def _prepare(attn_norm, attn_w_in, attn_w_out, ssm_norm, ssm_w_in, ssm_a_re, ssm_a_im, ssm_log_step, ssm_b_re,
             ssm_b_im, ssm_c_re, ssm_c_im, ssm_d, ssm_w_glu, ssm_b_glu, ssm_w_out, final_norm):
    attn_params = []
    for j in range(attn_norm.shape[0]):
        attn_params.append((attn_norm[j][None, :], _prep_attn_w_in(attn_w_in[j]), attn_w_out[j].astype(BF16)))
    ssm_params = []
    for j in range(ssm_norm.shape[0]):
        mats = _prep_ssm(ssm_a_re[j], ssm_a_im[j], ssm_log_step[j], ssm_b_re[j], ssm_b_im[j],
                         ssm_c_re[j], ssm_c_im[j], ssm_d[j])
        ssm_params.append((ssm_norm[j][None, :], ssm_w_in[j].T.astype(BF16), mats,
                           ssm_w_glu[j].T.astype(BF16), ssm_b_glu[j][:, None], ssm_w_out[j].astype(BF16)))
    return attn_params, ssm_params, final_norm[None, :]


def kernel(x_prompt, x_sample, attn_norm, attn_w_in, attn_w_out, ssm_norm, ssm_w_in, ssm_a_re, ssm_a_im, ssm_log_step, ssm_b_re, ssm_b_im, ssm_c_re, ssm_c_im, ssm_d, ssm_w_glu, ssm_b_glu, ssm_w_out, final_norm):
    attn_params, ssm_params, final_gain = _prepare(
        attn_norm, attn_w_in, attn_w_out, ssm_norm, ssm_w_in, ssm_a_re, ssm_a_im, ssm_log_step, ssm_b_re,
        ssm_b_im, ssm_c_re, ssm_c_im, ssm_d, ssm_w_glu, ssm_b_glu, ssm_w_out, final_norm)
    y_prompt = _trunk(x_prompt, attn_params, ssm_params, final_gain)
    y_sample = _trunk(x_sample, attn_params, ssm_params, final_gain)
    return (y_prompt, y_sample)
```

```python
import functools
import math

import jax
import jax.numpy as jnp
from jax import lax
from jax.experimental import pallas as pl
from jax.experimental.pallas import tpu as pltpu

F32 = jnp.float32
BF16 = jnp.bfloat16

D_MODEL = 1024
DEPTH = 4
GROUP_PATTERNS = ((128, 1), (512, 4), (2048, 16))
N_HEADS = 16
HEAD_DIM = 64
N_HEAD_PAIRS = N_HEADS // 2
LANES = 128
LANE_BLOCKS = D_MODEL // LANES
ROPE_THETA = 10000.0
SSM_GROUP = 16
SSM_GROUPS = D_MODEL // SSM_GROUP
SSM_STATE = 64
CHUNK = 16
CHUNK_COLS = CHUNK * SSM_GROUP
TILE = CHUNK * LANES
TQ = 128
NORM_EPS = 1e-6
VMEM_LIMIT = 56 * 1024 * 1024

ATTN_SLABS = {"q": (0, 4, 7), "k": (1, 5, 8), "v": (2, 6, 9), "z": 3}
N_ATTN_SLABS = 10


def _sigmoid(x):
    return 1.0 / (1.0 + jnp.exp(-x))


def _silu(x):
    return x * _sigmoid(x)


def _gelu_tanh(x):
    c = math.sqrt(2.0 / math.pi)
    return x * (0.5 * (1.0 + jnp.tanh(c * (x + 0.044715 * (x * x * x)))))


def _rmsnorm_rows(xf, gain):
    return xf * lax.rsqrt(jnp.mean(xf * xf, axis=-1, keepdims=True) + NORM_EPS) * gain


def _aligned(x, multiple):
    return x if isinstance(x, int) else pl.multiple_of(x, multiple)


def _clip(x, lo, hi):
    return min(max(x, lo), hi) if isinstance(x, int) else jnp.clip(x, lo, hi)


def _cmul(ar, ai, br, bi):
    return ar * br - ai * bi, ar * bi + ai * br


def _params(*semantics):
    return pltpu.CompilerParams(dimension_semantics=semantics, vmem_limit_bytes=VMEM_LIMIT)


def _stage_lane_blocks(x, slab_ref):
    for c in range(LANE_BLOCKS):
        slab_ref[c] = x[:, c * LANES:(c + 1) * LANES]


def _strided_rows(slab_ref, start, count, stride):
    return jnp.concatenate(
        [slab_ref[c, pl.ds(start, count, stride=stride), :] for c in range(LANE_BLOCKS)], axis=1)


def _attn_inproj_kernel(x_ref, g_ref, w_ref, cos_ref, sin_ref, o_ref, hn_ref, xs_ref):
    j = pl.program_id(1)
    gain = g_ref[...]

    @pl.when(j == 0)
    def _():
        x = x_ref[...]
        _stage_lane_blocks(x, xs_ref)
        hn_ref[...] = _rmsnorm_rows(x, gain).astype(BF16)

    for g in (1, 2):
        @pl.when(j == ATTN_SLABS["q"][g])
        def _(dilation=GROUP_PATTERNS[g][1]):
            rpc = TILE // dilation
            for r in range(dilation):
                rows = _strided_rows(xs_ref, r, rpc, dilation)
                hn_ref[r * rpc:(r + 1) * rpc, :] = _rmsnorm_rows(rows, gain).astype(BF16)

    half = TILE // 2
    for part in range(2):
        rows = slice(part * half, (part + 1) * half)
        acc = jnp.dot(hn_ref[rows, :], w_ref[...], preferred_element_type=F32)
        c = cos_ref[rows, :]
        s = sin_ref[rows, :]
        for hp in range(N_HEAD_PAIRS):
            t = acc[:, hp * LANES:(hp + 1) * LANES]
            o_ref[hp, rows, :] = (t * c + pltpu.roll(t, 64, axis=1) * s).astype(BF16)


def _attn_inproj(x2d, gain, w_p, cos_t, sin_t, seq_len):
    ntok = x2d.shape[0]
    assert seq_len % TILE == 0
    pos_blocks = seq_len // TILE

    def table_map(i, j):
        order = jnp.where(j >= ATTN_SLABS["q"][2], 2, jnp.where(j >= ATTN_SLABS["q"][1], 1, 0))
        is_plain = functools.reduce(jnp.logical_or, [j == b for b in ATTN_SLABS["v"] + (ATTN_SLABS["z"],)])
        return (jnp.where(is_plain, len(GROUP_PATTERNS), order), i % pos_blocks, 0)

    return pl.pallas_call(
        _attn_inproj_kernel,
        grid=(ntok // TILE, N_ATTN_SLABS),
        in_specs=[
            pl.BlockSpec((TILE, D_MODEL), lambda i, j: (i, 0)),
            pl.BlockSpec((1, D_MODEL), lambda i, j: (0, 0)),
            pl.BlockSpec((D_MODEL, D_MODEL), lambda i, j: (0, j)),
            pl.BlockSpec((None, TILE, LANES), table_map),
            pl.BlockSpec((None, TILE, LANES), table_map),
        ],
        out_specs=pl.BlockSpec((None, N_HEAD_PAIRS, TILE, LANES), lambda i, j: (j, 0, i, 0)),
        out_shape=jax.ShapeDtypeStruct((N_ATTN_SLABS, N_HEAD_PAIRS, ntok, LANES), BF16),
        scratch_shapes=[pltpu.VMEM((TILE, D_MODEL), BF16), pltpu.VMEM((LANE_BLOCKS, TILE, LANES), F32)],
        compiler_params=_params("parallel", "arbitrary"),
        name="attn_inproj",
    )(x2d, gain, w_p, cos_t, sin_t)


def _fused_attn_kernel(q0, k0, v0, z_ref, q1, k1, v1, q2, k2, v2, y_ref, acc_ref, m_ref, l_ref, bias_ref,
                       s_all, p_all, m_all, *, seq_len):
    lane = lax.broadcasted_iota(jnp.int32, (1, LANES), 1)
    head_of_lane = lax.rem(lane, 64) // 32
    first_head_out = lane < HEAD_DIM
    groups = ((q0, k0, v0), (q1, k1, v1), (q2, k2, v2))

    tk_max = bias_ref.shape[2]
    row_i = lax.rem(lax.broadcasted_iota(jnp.int32, (2 * TQ, tk_max), 0), TQ)
    col_i = lax.broadcasted_iota(jnp.int32, (2 * TQ, tk_max), 1)
    for o in range(bias_ref.shape[0]):
        bias_ref[o] = jnp.where(jnp.abs(row_i - col_i + o * 64) <= 64, 0.0, -jnp.inf).astype(F32)

    n_windows = seq_len // TILE

    def window(w, carry):
        base = _aligned(w * TILE, TILE)
        for g, (q_ref, k_ref, v_ref) in enumerate(groups):
            win, dilation = GROUP_PATTERNS[g]
            half = win // (2 * dilation)
            length = seq_len // dilation
            rpc = TILE // dilation
            nqb = rpc // TQ
            tk = min(4 * half, length)
            n_pieces = tk // half
            n_blocks = dilation * nqb

            def locate(idx, dilation=dilation, half=half, length=length, rpc=rpc, nqb=nqb, tk=tk):
                r = idx // nqb
                lw = (idx % nqb) * TQ
                l0 = w * rpc + lw
                ks = _clip(l0 - half, 0, length - tk)
                return r, lw, l0, ks

            def key_rows(ref, r, ks, rpc=rpc, half=half, n_pieces=n_pieces):
                parts = []
                for p in range(n_pieces):
                    l = ks + p * half
                    row = _aligned((l // rpc) * TILE + r * rpc + l % rpc, half)
                    parts.append(ref[pl.ds(row, half), :])
                return jnp.concatenate(parts, axis=0)

            def scores(idx, q_ref=q_ref, k_ref=k_ref, rpc=rpc, half=half, tk=tk):
                r, lw, l0, ks = locate(idx)
                q = q_ref[pl.ds(_aligned(base + r * rpc + lw, TQ), TQ), :]
                k = key_rows(k_ref, r, ks)
                zero = jnp.zeros_like(q)
                q_heads = jnp.concatenate([jnp.where(head_of_lane == 0, q, zero),
                                           jnp.where(head_of_lane == 1, q, zero)], axis=0)
                s_all[idx, :, 0:tk] = lax.dot_general(q_heads, k, (((1,), (1,)), ((), ())),
                                                      preferred_element_type=F32)

            def softmax(idx, half=half, tk=tk):
                _, _, l0, ks = locate(idx)
                s = s_all[idx, :, 0:tk] + bias_ref[(l0 - ks) // half][:, 0:tk]
                m2 = jnp.max(s, axis=-1, keepdims=True)
                p_all[idx, :, 0:tk] = jnp.exp(s - m2).astype(BF16)
                m_all[idx] = jnp.where(first_head_out, m2[:TQ], m2[TQ:])

            def values(idx, g=g, v_ref=v_ref, dilation=dilation, tk=tk):
                r, lw, l0, ks = locate(idx)
                v = key_rows(v_ref, r, ks)
                v_aug = jnp.concatenate([v, jnp.ones_like(v)], axis=1)
                pv2 = jnp.dot(p_all[idx, :, 0:tk], v_aug, preferred_element_type=F32)
                pv = jnp.where(first_head_out, pv2[:TQ, :LANES], pv2[TQ:, :LANES])
                dd = jnp.where(first_head_out, pv2[:TQ, LANES:], pv2[TQ:, LANES:])
                mm = m_all[idx]
                if dilation == 1:
                    rows = pl.ds(_aligned(lw, TQ), TQ)
                else:
                    rows = pl.ds(lw * dilation + r, TQ, stride=dilation)
                if g == 0:
                    acc_ref[rows, :] = pv
                    m_ref[rows, :] = mm
                    l_ref[rows, :] = dd
                else:
                    m_old = m_ref[rows, :]
                    m_new = jnp.maximum(m_old, mm)
                    e_old = jnp.exp(m_old - m_new)
                    e_new = jnp.exp(mm - m_new)
                    acc_ref[rows, :] = acc_ref[rows, :] * e_old + pv * e_new
                    l_ref[rows, :] = l_ref[rows, :] * e_old + dd * e_new
                    if g + 1 < len(groups):
                        m_ref[rows, :] = m_new

            for phase in (scores, softmax, values):
                for idx in range(n_blocks):
                    phase(idx)
        gate = _silu(z_ref[pl.ds(base, TILE), :].astype(F32))
        y_ref[pl.ds(base, TILE), :] = (acc_ref[...] / l_ref[...] * gate).astype(BF16)
        return carry

    if n_windows == 1:
        window(0, 0)
    else:
        lax.fori_loop(0, n_windows, window, 0)


def _fused_attention(qkvz, batch, seq_len):
    view = qkvz.reshape(N_ATTN_SLABS, N_HEAD_PAIRS, batch, seq_len, LANES)
    slab_bytes = seq_len * LANES * 2
    mode = {} if 2 * N_ATTN_SLABS * slab_bytes <= VMEM_LIMIT // 2 else {"pipeline_mode": pl.Buffered(1)}

    def slab_spec(slab):
        return pl.BlockSpec((None, None, None, seq_len, LANES), lambda b, hp: (slab, hp, b, 0, 0), **mode)

    order = [ATTN_SLABS["q"][0], ATTN_SLABS["k"][0], ATTN_SLABS["v"][0], ATTN_SLABS["z"],
             ATTN_SLABS["q"][1], ATTN_SLABS["k"][1], ATTN_SLABS["v"][1],
             ATTN_SLABS["q"][2], ATTN_SLABS["k"][2], ATTN_SLABS["v"][2]]
    tk_max = min(4 * 64, seq_len // GROUP_PATTERNS[0][1])
    y = pl.pallas_call(
        functools.partial(_fused_attn_kernel, seq_len=seq_len),
        grid=(batch, N_HEAD_PAIRS),
        in_specs=[slab_spec(s) for s in order],
        out_specs=pl.BlockSpec((None, None, seq_len, LANES), lambda b, hp: (hp, b, 0, 0)),
        out_shape=jax.ShapeDtypeStruct((N_HEAD_PAIRS, batch, seq_len, LANES), BF16),
        scratch_shapes=[pltpu.VMEM((TILE, LANES), F32)] * 3 + [
            pltpu.VMEM((3, 2 * TQ, tk_max), F32),
            pltpu.VMEM((TILE // TQ, 2 * TQ, tk_max), F32),
            pltpu.VMEM((TILE // TQ, 2 * TQ, tk_max), BF16),
            pltpu.VMEM((TILE // TQ, TQ, LANES), F32)],
        compiler_params=_params("parallel", "parallel"),
        name="fused_banded_attn",
    )(*([view] * N_ATTN_SLABS))
    return y.reshape(N_HEAD_PAIRS, batch * seq_len, LANES)


def _attn_out_kernel(y_ref, x_ref, w_ref, out_ref):
    y = jnp.concatenate([y_ref[hp] for hp in range(N_HEAD_PAIRS)], axis=1)
    out_ref[...] = x_ref[...] + jnp.dot(y, w_ref[...], preferred_element_type=F32)


def _attn_out(y, x2d, w_out):
    ntok = x2d.shape[0]
    tm = 512
    return pl.pallas_call(
        _attn_out_kernel,
        grid=(ntok // tm,),
        in_specs=[
            pl.BlockSpec((N_HEAD_PAIRS, tm, LANES), lambda i: (0, i, 0)),
            pl.BlockSpec((tm, D_MODEL), lambda i: (i, 0)),
            pl.BlockSpec((D_MODEL, D_MODEL), lambda i: (0, 0)),
        ],
        out_specs=pl.BlockSpec((tm, D_MODEL), lambda i: (i, 0)),
        out_shape=jax.ShapeDtypeStruct((ntok, D_MODEL), F32),
        compiler_params=_params("parallel"),
        name="attn_outproj",
    )(y, x2d, w_out)


def _ssm_inproj_kernel(x_ref, g_ref, wt_ref, o_ref, hn_ref, xs_ref):
    @pl.when(pl.program_id(1) == 0)
    def _():
        _stage_lane_blocks(x_ref[...], xs_ref)
        gain = g_ref[...]
        for t in range(CHUNK):
            rows = _strided_rows(xs_ref, t, LANES, CHUNK)
            hn_ref[t * LANES:(t + 1) * LANES, :] = _rmsnorm_rows(rows, gain).astype(BF16)

    half = TILE // 2
    for part in range(2):
        cols = slice(part * half, (part + 1) * half)
        o_ref[:, cols] = lax.dot_general(wt_ref[...], hn_ref[cols, :], (((1,), (1,)), ((), ())),
                                         preferred_element_type=F32).astype(BF16)


def _ssm_inproj(x2d, gain, w_in_t):
    ntok = x2d.shape[0]
    return pl.pallas_call(
        _ssm_inproj_kernel,
        grid=(ntok // TILE, 2),
        in_specs=[
            pl.BlockSpec((TILE, D_MODEL), lambda i, j: (i, 0)),
            pl.BlockSpec((1, D_MODEL), lambda i, j: (0, 0)),
            pl.BlockSpec((D_MODEL, D_MODEL), lambda i, j: (j, 0)),
        ],
        out_specs=pl.BlockSpec((None, D_MODEL, TILE), lambda i, j: (j, 0, i)),
        out_shape=jax.ShapeDtypeStruct((2, D_MODEL, ntok), BF16),
        scratch_shapes=[pltpu.VMEM((TILE, D_MODEL), BF16), pltpu.VMEM((LANE_BLOCKS, TILE, LANES), F32)],
        compiler_params=_params("parallel", "arbitrary"),
        name="ssm_inproj",
    )(x2d, gain, w_in_t)


def _ssm_core_kernel(u_ref, mt_ref, w_ref, vt_ref, a16_ref, apl_ref, y_ref, ut_ref, s_ref, hf_ref, hb_ref,
                     *, n_seq, chunks_per_seq):
    n_tiles = n_seq * chunks_per_seq // LANES
    p = SSM_STATE
    sub = 8
    for tile in range(n_tiles):
        for t in range(CHUNK):
            src = (tile * CHUNK + t) * LANES
            ut_ref[t * SSM_GROUP:(t + 1) * SSM_GROUP, tile * LANES:(tile + 1) * LANES] = u_ref[:, src:src + LANES]
    ut = ut_ref[...]
    s_ref[...] = lax.dot_general(ut, w_ref[...], (((0,), (0,)), ((), ())), preferred_element_type=F32)

    lane = lax.broadcasted_iota(jnp.int32, (sub, LANES), 1)
    row = lax.broadcasted_iota(jnp.int32, (sub, LANES), 0)
    is_f = lane < p
    steps_in = jnp.where(is_f, row, sub - 1 - row)
    cur = (jnp.broadcast_to(a16_ref[0:1, :], (sub, LANES)), jnp.broadcast_to(a16_ref[1:2, :], (sub, LANES)))
    levels = []
    shift = 1
    while shift < sub:
        levels.append((shift, steps_in >= shift, cur))
        cur = _cmul(cur[0], cur[1], cur[0], cur[1])
        shift *= 2
    apl_re = apl_ref[0]
    apl_im = apl_ref[1]
    inner = steps_in >= 1
    n_steps = chunks_per_seq // sub

    def toward_scan_start(x, by):
        return jnp.where(is_f, pltpu.roll(x, by, axis=0), pltpu.roll(x, sub - by, axis=0))

    def seq_body(b, carry):
        seq0 = b * chunks_per_seq

        def step(i, h):
            h_re, h_im = h
            rf = pl.multiple_of(seq0 + i * sub, sub)
            rb = pl.multiple_of(seq0 + (n_steps - 1 - i) * sub, sub)
            x_re = jnp.where(is_f, s_ref[pl.ds(rf, sub), 0:LANES], s_ref[pl.ds(rb, sub), 0:LANES])
            x_im = jnp.where(is_f, s_ref[pl.ds(rf, sub), LANES:2 * LANES], s_ref[pl.ds(rb, sub), LANES:2 * LANES])
            for by, keep, (lr, li) in levels:
                d_re, d_im = _cmul(lr, li, jnp.where(keep, toward_scan_start(x_re, by), 0.0),
                                   jnp.where(keep, toward_scan_start(x_im, by), 0.0))
                x_re = x_re + d_re
                x_im = x_im + d_im
            c_re, c_im = _cmul(apl_re, apl_im, h_re, h_im)
            t_re = x_re + c_re
            t_im = x_im + c_im
            h0_re = jnp.where(inner, toward_scan_start(t_re, 1), h_re)
            h0_im = jnp.where(inner, toward_scan_start(t_im, 1), h_im)
            hf_ref[pl.ds(rf, sub), 0:LANES] = h0_re
            hf_ref[pl.ds(rf, sub), LANES:2 * LANES] = h0_im
            hb_ref[pl.ds(rb, sub), 0:LANES] = h0_re
            hb_ref[pl.ds(rb, sub), LANES:2 * LANES] = h0_im
            n_re = jnp.where(is_f[0:1], t_re[sub - 1:sub], t_re[0:1])
            n_im = jnp.where(is_f[0:1], t_im[sub - 1:sub], t_im[0:1])
            return n_re, n_im

        zero = jnp.zeros((1, LANES), F32)
        lax.fori_loop(0, n_steps, step, (zero, zero), unroll=4)
        return carry

    lax.fori_loop(0, n_seq, seq_body, 0)
    is_f_wide = lax.rem(lax.broadcasted_iota(jnp.int32, (1, 2 * LANES), 1), LANES) < p
    h0 = jnp.where(is_f_wide, hf_ref[...], hb_ref[...]).astype(BF16)
    yt = (jnp.dot(mt_ref[...], ut, preferred_element_type=F32)
          + lax.dot_general(vt_ref[...], h0, (((1,), (1,)), ((), ())), preferred_element_type=F32)).astype(BF16)
    for tile in range(n_tiles):
        for t in range(CHUNK):
            dst = (tile * CHUNK + t) * LANES
            y_ref[:, dst:dst + LANES] = yt[t * SSM_GROUP:(t + 1) * SSM_GROUP, tile * LANES:(tile + 1) * LANES]


def _ssm_core(uz_t, mats, batch, seq_len):
    m_t, w_n, v_t, a16, apl = mats
    ntok = batch * seq_len
    n_chunks = ntok // CHUNK
    mat_spec = pl.BlockSpec((None, CHUNK_COLS, CHUNK_COLS), lambda g: (g, 0, 0))
    return pl.pallas_call(
        functools.partial(_ssm_core_kernel, n_seq=batch, chunks_per_seq=seq_len // CHUNK),
        grid=(SSM_GROUPS,),
        in_specs=[pl.BlockSpec((None, SSM_GROUP, ntok), lambda g: (0, g, 0)),
                  mat_spec, mat_spec, mat_spec,
                  pl.BlockSpec((None, 2, LANES), lambda g: (g, 0, 0)),
                  pl.BlockSpec((None, 2, 8, LANES), lambda g: (g, 0, 0, 0))],
        out_specs=pl.BlockSpec((SSM_GROUP, ntok), lambda g: (g, 0)),
        out_shape=jax.ShapeDtypeStruct((D_MODEL, ntok), BF16),
        scratch_shapes=[pltpu.VMEM((CHUNK_COLS, n_chunks), BF16)] + [pltpu.VMEM((n_chunks, 2 * LANES), F32)] * 3,
        compiler_params=_params("parallel"),
        name="ssm_chunk_scan",
    )(uz_t, m_t, w_n, v_t, a16, apl)


def _ssm_glu_kernel(y_ref, z_ref, wg_ref, bg_ref, o_ref):
    gl = _gelu_tanh(y_ref[...].astype(F32))
    t = jnp.dot(wg_ref[...], gl.astype(BF16), preferred_element_type=F32) + bg_ref[...]
    y = gl * _sigmoid(t)
    o_ref[...] = (y * _silu(z_ref[...].astype(F32))).astype(BF16)


def _ssm_glu(y_t, uz_t, w_glu_t, b_glu_col):
    ntok = y_t.shape[1]
    tl = 1024
    return pl.pallas_call(
        _ssm_glu_kernel,
        grid=(ntok // tl,),
        in_specs=[pl.BlockSpec((D_MODEL, tl), lambda i: (0, i)),
                  pl.BlockSpec((None, D_MODEL, tl), lambda i: (1, 0, i)),
                  pl.BlockSpec((D_MODEL, D_MODEL), lambda i: (0, 0)),
                  pl.BlockSpec((D_MODEL, 1), lambda i: (0, 0))],
        out_specs=pl.BlockSpec((D_MODEL, tl), lambda i: (0, i)),
        out_shape=jax.ShapeDtypeStruct((D_MODEL, ntok), BF16),
        compiler_params=_params("parallel"),
        name="ssm_glu",
    )(y_t, uz_t, w_glu_t, b_glu_col)


def _ssm_out_kernel(y_ref, x_ref, w_ref, fg_ref, out_ref, xs_ref, *, final):
    _stage_lane_blocks(x_ref[...], xs_ref)
    half = TILE // 2
    for part in range(2):
        res = lax.dot_general(y_ref[:, part * half:(part + 1) * half], w_ref[...], (((0,), (0,)), ((), ())),
                              preferred_element_type=F32)
        for tt in range(CHUNK // 2):
            rows = pl.ds(part * (CHUNK // 2) + tt, LANES, stride=CHUNK)
            for c in range(LANE_BLOCKS):
                xs_ref[c, rows, :] = xs_ref[c, rows, :] + res[tt * LANES:(tt + 1) * LANES, c * LANES:(c + 1) * LANES]
    out = jnp.concatenate([xs_ref[c] for c in range(LANE_BLOCKS)], axis=1)
    if final:
        out = _rmsnorm_rows(out, fg_ref[...])
    out_ref[...] = out


def _ssm_out(y3_t, x2d, w_out, final_gain, final):
    ntok = x2d.shape[0]
    return pl.pallas_call(
        functools.partial(_ssm_out_kernel, final=final),
        grid=(ntok // TILE,),
        in_specs=[pl.BlockSpec((D_MODEL, TILE), lambda i: (0, i)),
                  pl.BlockSpec((TILE, D_MODEL), lambda i: (i, 0), pipeline_mode=pl.Buffered(1)),
                  pl.BlockSpec((D_MODEL, D_MODEL), lambda i: (0, 0)),
                  pl.BlockSpec((1, D_MODEL), lambda i: (0, 0))],
        out_specs=pl.BlockSpec((TILE, D_MODEL), lambda i: (i, 0)),
        out_shape=jax.ShapeDtypeStruct((ntok, D_MODEL), F32),
        scratch_shapes=[pltpu.VMEM((LANE_BLOCKS, TILE, LANES), F32)],
        compiler_params=_params("parallel"),
        name="ssm_outproj",
    )(y3_t, x2d, w_out, final_gain)


def _pair_heads(w):
    rows = w.shape[0]
    return w.reshape(rows, N_HEAD_PAIRS, 2, 2, HEAD_DIM // 2).transpose(0, 1, 3, 2, 4).reshape(rows, -1)


def _prep_attn_w_in(w_in):
    width = N_HEADS * HEAD_DIM
    blocks = [None] * N_ATTN_SLABS
    for g in range(len(GROUP_PATTERNS)):
        base = 3 * g * width
        blocks[ATTN_SLABS["q"][g]] = _pair_heads(w_in[:, base:base + width]) * (HEAD_DIM ** -0.5)
        blocks[ATTN_SLABS["k"][g]] = _pair_heads(w_in[:, base + width:base + 2 * width])
        blocks[ATTN_SLABS["v"][g]] = w_in[:, base + 2 * width:base + 3 * width]
    blocks[ATTN_SLABS["z"]] = w_in[:, 3 * len(GROUP_PATTERNS) * width:]
    return jnp.concatenate(blocks, axis=1).astype(BF16)


def _rope_tables(seq_len):
    inv_freq = ROPE_THETA ** (-jnp.arange(0, HEAD_DIM, 2, dtype=F32) / HEAD_DIM)
    ang = jnp.arange(seq_len, dtype=F32)[:, None] * inv_freq[None, :]
    cos, sin = jnp.cos(ang), jnp.sin(ang)
    cos_t = jnp.concatenate([cos] * 4, axis=1)
    sin_t = jnp.concatenate([-sin, -sin, sin, sin], axis=1)

    def orders(tab, identity):
        out = []
        for _, dilation in GROUP_PATTERNS:
            rpc = TILE // dilation
            out.append(tab.reshape(seq_len // TILE, rpc, dilation, LANES).transpose(0, 2, 1, 3).reshape(seq_len, LANES))
        out.append(jnp.full_like(tab, identity))
        return jnp.stack(out)

    return orders(cos_t, 1.0), orders(sin_t, 0.0)


def _prep_ssm(a_re, a_im, log_step, b_re, b_im, c_re, c_im, d):
    hi = lax.Precision.HIGHEST
    g_n, p_n, c_n, t_n = SSM_GROUPS, SSM_STATE, SSM_GROUP, CHUNK
    dt = jnp.exp(log_step)[..., None]
    ldr = a_re * dt
    ldi = a_im * dt
    abar_re = jnp.exp(ldr) * jnp.cos(ldi)
    abar_im = jnp.exp(ldr) * jnp.sin(ldi)
    den = a_re * a_re + a_im * a_im
    nr = abar_re - 1.0
    ni = abar_im
    f_re = (nr * a_re + ni * a_im) / den
    f_im = (ni * a_re - nr * a_im) / den
    bb_re = f_re[..., None] * b_re - f_im[..., None] * b_im
    bb_im = f_re[..., None] * b_im + f_im[..., None] * b_re
    n = jnp.arange(t_n + 1, dtype=F32)[:, None, None, None]
    pw_re = jnp.exp(n * ldr) * jnp.cos(n * ldi)
    pw_im = jnp.exp(n * ldr) * jnp.sin(n * ldi)
    ca_re = c_re[None] * pw_re[:, :, :, None, :] - c_im[None] * pw_im[:, :, :, None, :]
    ca_im = c_re[None] * pw_im[:, :, :, None, :] + c_im[None] * pw_re[:, :, :, None, :]
    kern = (jnp.einsum('ldgcp,dgpk->ldgck', ca_re, bb_re, precision=hi)
            - jnp.einsum('ldgcp,dgpk->ldgck', ca_im, bb_im, precision=hi))
    s_idx = jnp.arange(t_n)[:, None]
    t_idx = jnp.arange(t_n)[None, :]
    lag = t_idx - s_idx
    m_f = kern[jnp.clip(lag, 0, t_n - 1), 0] * (lag >= 0)[:, :, None, None, None].astype(F32)
    m_b = kern[jnp.clip(-lag, 0, t_n - 1), 1] * (lag <= 0)[:, :, None, None, None].astype(F32)
    m_mat = (m_f + m_b).transpose(2, 0, 4, 1, 3).reshape(g_n, CHUNK_COLS, CHUNK_COLS)
    skip = jnp.tile(d.reshape(g_n, c_n), (1, t_n))
    m_mat = m_mat + jnp.eye(CHUNK_COLS, dtype=F32)[None] * skip[:, None, :]

    def state_in(pw_r, pw_i, direction):
        re = pw_r[..., None] * bb_re[direction][None] - pw_i[..., None] * bb_im[direction][None]
        im = pw_r[..., None] * bb_im[direction][None] + pw_i[..., None] * bb_re[direction][None]
        to_cols = lambda x: x.transpose(1, 0, 3, 2).reshape(g_n, CHUNK_COLS, p_n)
        return to_cols(re), to_cols(im)

    rev = t_n - 1 - jnp.arange(t_n)
    wf_re, wf_im = state_in(pw_re[rev, 0], pw_im[rev, 0], 0)
    wb_re, wb_im = state_in(pw_re[:t_n, 1], pw_im[:t_n, 1], 1)
    w_mat = jnp.concatenate([wf_re, wb_re, wf_im, wb_im], axis=2)

    def state_out(ca_r, ca_i):
        to_rows = lambda x: x.transpose(1, 3, 0, 2).reshape(g_n, p_n, CHUNK_COLS)
        return to_rows(ca_r), to_rows(-ca_i)

    vf_re, vf_im = state_out(ca_re[1:t_n + 1, 0], ca_im[1:t_n + 1, 0])
    back = t_n - jnp.arange(t_n)
    vb_re, vb_im = state_out(ca_re[back, 1], ca_im[back, 1])
    v_mat = jnp.concatenate([vf_re, vb_re, vf_im, vb_im], axis=1)

    lanes = lambda f, b: jnp.concatenate([f, b], axis=-1)
    a16 = jnp.stack([lanes(pw_re[t_n, 0], pw_re[t_n, 1]), lanes(pw_im[t_n, 0], pw_im[t_n, 1])], axis=1)
    j_row = jnp.arange(8, dtype=F32)[None, :, None]
    n_f = t_n * (j_row + 1.0)
    n_b = t_n * (8.0 - j_row)
    pow_rows = lambda nn, dr: (jnp.exp(nn * ldr[dr][:, None, :]) * jnp.cos(nn * ldi[dr][:, None, :]),
                               jnp.exp(nn * ldr[dr][:, None, :]) * jnp.sin(nn * ldi[dr][:, None, :]))
    plf_re, plf_im = pow_rows(n_f, 0)
    plb_re, plb_im = pow_rows(n_b, 1)
    apl = jnp.stack([lanes(plf_re, plb_re), lanes(plf_im, plb_im)], axis=1)
    tr = lambda x: x.transpose(0, 2, 1).astype(BF16)
    return tr(m_mat), w_mat.astype(BF16), tr(v_mat), a16, apl


def _attention_layer(x2d, batch, seq_len, params, rope):
    gain, w_in_p, w_out = params
    qkvz = _attn_inproj(x2d, gain, w_in_p, rope[0], rope[1], seq_len)
    y = _fused_attention(qkvz, batch, seq_len)
    return _attn_out(y, x2d, w_out)


def _ssm_layer(x2d, batch, seq_len, params, final_gain, final):
    gain, w_in_t, mats, w_glu_t, b_glu_col, w_out = params
    uz_t = _ssm_inproj(x2d, gain, w_in_t)
    y_t = _ssm_core(uz_t, mats, batch, seq_len)
    y3_t = _ssm_glu(y_t, uz_t, w_glu_t, b_glu_col)
    return _ssm_out(y3_t, x2d, w_out, final_gain, final)


def _trunk(x, attn_params, ssm_params, final_gain):
    batch, seq_len, _ = x.shape
    assert seq_len % TILE == 0
    rope = _rope_tables(seq_len)
    x2d = x.reshape(batch * seq_len, D_MODEL)
    for i in range(DEPTH):
        j = i // 2
        if i % 2 == 0:
            x2d = _attention_layer(x2d, batch, seq_len, attn_params[j], rope)
        else:
            x2d = _ssm_layer(x2d, batch, seq_len, ssm_params[j], final_gain, final=(i == DEPTH - 1))
    return x2d.reshape(batch, seq_len, D_MODEL)


def _prepare(attn_norm, attn_w_in, attn_w_out, ssm_norm, ssm_w_in, ssm_a_re, ssm_a_im, ssm_log_step, ssm_b_re,
             ssm_b_im, ssm_c_re, ssm_c_im, ssm_d, ssm_w_glu, ssm_b_glu, ssm_w_out, final_norm):
    attn_params = []
    for j in range(attn_norm.shape[0]):
        attn_params.append((attn_norm[j][None, :], _prep_attn_w_in(attn_w_in[j]), attn_w_out[j].astype(BF16)))
    ssm_params = []
    for j in range(ssm_norm.shape[0]):
        mats = _prep_ssm(ssm_a_re[j], ssm_a_im[j], ssm_log_step[j], ssm_b_re[j], ssm_b_im[j],
                         ssm_c_re[j], ssm_c_im[j], ssm_d[j])
        ssm_params.append((ssm_norm[j][None, :], ssm_w_in[j].T.astype(BF16), mats,
                           ssm_w_glu[j].T.astype(BF16), ssm_b_glu[j][:, None], ssm_w_out[j].astype(BF16)))
    return attn_params, ssm_params, final_norm[None, :]


def kernel(x_prompt, x_sample, attn_norm, attn_w_in, attn_w_out, ssm_norm, ssm_w_in, ssm_a_re, ssm_a_im, ssm_log_step, ssm_b_re, ssm_b_im, ssm_c_re, ssm_c_im, ssm_d, ssm_w_glu, ssm_b_glu, ssm_w_out, final_norm):
    attn_params, ssm_params, final_gain = _prepare(
        attn_norm, attn_w_in, attn_w_out, ssm_norm, ssm_w_in, ssm_a_re, ssm_a_im, ssm_log_step, ssm_b_re,
        ssm_b_im, ssm_c_re, ssm_c_im, ssm_d, ssm_w_glu, ssm_b_glu, ssm_w_out, final_norm)
    y_prompt = _trunk(x_prompt, attn_params, ssm_params, final_gain)
    y_sample = _trunk(x_sample, attn_params, ssm_params, final_gain)
    return (y_prompt, y_sample)
```

```python
import functools
import math

import jax
import jax.numpy as jnp
from jax import lax
from jax.experimental import pallas as pl
from jax.experimental.pallas import tpu as pltpu

F32 = jnp.float32
BF16 = jnp.bfloat16

D_MODEL = 1024
DEPTH = 4
GROUP_PATTERNS = ((128, 1), (512, 4), (2048, 16))
N_HEADS = 16
HEAD_DIM = 64
N_HEAD_PAIRS = N_HEADS // 2
LANES = 128
LANE_BLOCKS = D_MODEL // LANES
ROPE_THETA = 10000.0
SSM_GROUP = 16
SSM_GROUPS = D_MODEL // SSM_GROUP
SSM_STATE = 64
CHUNK = 16
CHUNK_COLS = CHUNK * SSM_GROUP
TILE = CHUNK * LANES
TQ = 128
NORM_EPS = 1e-6
VMEM_LIMIT = 56 * 1024 * 1024

ATTN_SLABS = {"q": (0, 4, 7), "k": (1, 5, 8), "v": (2, 6, 9), "z": 3}
N_ATTN_SLABS = 10


def _sigmoid(x):
    return 1.0 / (1.0 + jnp.exp(-x))


def _silu(x):
    return x * _sigmoid(x)


def _gelu_tanh(x):
    c = math.sqrt(2.0 / math.pi)
    return x * (0.5 * (1.0 + jnp.tanh(c * (x + 0.044715 * (x * x * x)))))


def _rmsnorm_rows(xf, gain):
    return xf * lax.rsqrt(jnp.mean(xf * xf, axis=-1, keepdims=True) + NORM_EPS) * gain


def _aligned(x, multiple):
    return x if isinstance(x, int) else pl.multiple_of(x, multiple)


def _clip(x, lo, hi):
    return min(max(x, lo), hi) if isinstance(x, int) else jnp.clip(x, lo, hi)


def _cmul(ar, ai, br, bi):
    return ar * br - ai * bi, ar * bi + ai * br


def _params(*semantics):
    return pltpu.CompilerParams(dimension_semantics=semantics, vmem_limit_bytes=VMEM_LIMIT)


def _stage_lane_blocks(x, slab_ref):
    for c in range(LANE_BLOCKS):
        slab_ref[c] = x[:, c * LANES:(c + 1) * LANES]


def _strided_rows(slab_ref, start, count, stride):
    return jnp.concatenate(
        [slab_ref[c, pl.ds(start, count, stride=stride), :] for c in range(LANE_BLOCKS)], axis=1)


def _attn_inproj_kernel(x_ref, g_ref, w_ref, cos_ref, sin_ref, o_ref, hn_ref, xs_ref):
    j = pl.program_id(1)

    @pl.when(j == 0)
    def _():
        hn = _rmsnorm_rows(x_ref[...], g_ref[...])
        _stage_lane_blocks(hn, xs_ref)
        hn_ref[...] = hn.astype(BF16)

    for g in (1, 2):
        @pl.when(j == ATTN_SLABS["q"][g])
        def _(dilation=GROUP_PATTERNS[g][1]):
            rpc = TILE // dilation
            for r in range(dilation):
                hn_ref[r * rpc:(r + 1) * rpc, :] = _strided_rows(xs_ref, r, rpc, dilation).astype(BF16)

    half = TILE // 2
    for part in range(2):
        rows = slice(part * half, (part + 1) * half)
        acc = jnp.dot(hn_ref[rows, :], w_ref[...], preferred_element_type=F32)
        c = cos_ref[rows, :]
        s = sin_ref[rows, :]
        for hp in range(N_HEAD_PAIRS):
            t = acc[:, hp * LANES:(hp + 1) * LANES]
            o_ref[hp, rows, :] = (t * c + pltpu.roll(t, 64, axis=1) * s).astype(BF16)


def _attn_inproj(x2d, gain, w_p, cos_t, sin_t, seq_len):
    ntok = x2d.shape[0]
    assert seq_len % TILE == 0
    pos_blocks = seq_len // TILE

    def table_map(i, j):
        order = jnp.where(j >= ATTN_SLABS["q"][2], 2, jnp.where(j >= ATTN_SLABS["q"][1], 1, 0))
        is_plain = functools.reduce(jnp.logical_or, [j == b for b in ATTN_SLABS["v"] + (ATTN_SLABS["z"],)])
        return (jnp.where(is_plain, len(GROUP_PATTERNS), order), i % pos_blocks, 0)

    return pl.pallas_call(
        _attn_inproj_kernel,
        grid=(ntok // TILE, N_ATTN_SLABS),
        in_specs=[
            pl.BlockSpec((TILE, D_MODEL), lambda i, j: (i, 0)),
            pl.BlockSpec((1, D_MODEL), lambda i, j: (0, 0)),
            pl.BlockSpec((D_MODEL, D_MODEL), lambda i, j: (0, j)),
            pl.BlockSpec((None, TILE, LANES), table_map),
            pl.BlockSpec((None, TILE, LANES), table_map),
        ],
        out_specs=pl.BlockSpec((None, N_HEAD_PAIRS, TILE, LANES), lambda i, j: (j, 0, i, 0)),
        out_shape=jax.ShapeDtypeStruct((N_ATTN_SLABS, N_HEAD_PAIRS, ntok, LANES), BF16),
        scratch_shapes=[pltpu.VMEM((TILE, D_MODEL), BF16), pltpu.VMEM((LANE_BLOCKS, TILE, LANES), F32)],
        compiler_params=_params("parallel", "arbitrary"),
        name="attn_inproj",
    )(x2d, gain, w_p, cos_t, sin_t)


def _fused_attn_kernel(q0, k0, v0, z_ref, q1, k1, v1, q2, k2, v2, y_ref, acc_ref, m_ref, l_ref, bias_ref,
                       s_all, p_all, m_all, *, seq_len):
    lane = lax.broadcasted_iota(jnp.int32, (1, LANES), 1)
    head_of_lane = lax.rem(lane, 64) // 32
    first_head_out = lane < HEAD_DIM
    groups = ((q0, k0, v0), (q1, k1, v1), (q2, k2, v2))

    tk_max = bias_ref.shape[2]
    row_i = lax.rem(lax.broadcasted_iota(jnp.int32, (2 * TQ, tk_max), 0), TQ)
    col_i = lax.broadcasted_iota(jnp.int32, (2 * TQ, tk_max), 1)
    for o in range(bias_ref.shape[0]):
        bias_ref[o] = jnp.where(jnp.abs(row_i - col_i + o * 64) <= 64, 0.0, -jnp.inf).astype(F32)

    n_windows = seq_len // TILE

    def window(w, carry):
        base = _aligned(w * TILE, TILE)
        for g, (q_ref, k_ref, v_ref) in enumerate(groups):
            win, dilation = GROUP_PATTERNS[g]
            half = win // (2 * dilation)
            length = seq_len // dilation
            rpc = TILE // dilation
            nqb = rpc // TQ
            tk = min(4 * half, length)
            n_pieces = tk // half
            n_blocks = dilation * nqb

            def locate(idx, dilation=dilation, half=half, length=length, rpc=rpc, nqb=nqb, tk=tk):
                r = idx // nqb
                lw = (idx % nqb) * TQ
                l0 = w * rpc + lw
                ks = _clip(l0 - half, 0, length - tk)
                return r, lw, l0, ks

            def key_rows(ref, r, ks, rpc=rpc, half=half, n_pieces=n_pieces):
                parts = []
                for p in range(n_pieces):
                    l = ks + p * half
                    row = _aligned((l // rpc) * TILE + r * rpc + l % rpc, half)
                    parts.append(ref[pl.ds(row, half), :])
                return jnp.concatenate(parts, axis=0)

            def scores(idx, q_ref=q_ref, k_ref=k_ref, rpc=rpc, half=half, tk=tk):
                r, lw, l0, ks = locate(idx)
                q = q_ref[pl.ds(_aligned(base + r * rpc + lw, TQ), TQ), :]
                k = key_rows(k_ref, r, ks)
                zero = jnp.zeros_like(q)
                q_heads = jnp.concatenate([jnp.where(head_of_lane == 0, q, zero),
                                           jnp.where(head_of_lane == 1, q, zero)], axis=0)
                s_all[idx, :, 0:tk] = lax.dot_general(q_heads, k, (((1,), (1,)), ((), ())),
                                                      preferred_element_type=F32)

            def softmax(idx, half=half, tk=tk):
                _, _, l0, ks = locate(idx)
                s = s_all[idx, :, 0:tk] + bias_ref[(l0 - ks) // half][:, 0:tk]
                m2 = jnp.max(s, axis=-1, keepdims=True)
                p_all[idx, :, 0:tk] = jnp.exp(s - m2).astype(BF16)
                m_all[idx] = jnp.where(first_head_out, m2[:TQ], m2[TQ:])

            def values(idx, g=g, v_ref=v_ref, dilation=dilation, tk=tk):
                r, lw, l0, ks = locate(idx)
                v = key_rows(v_ref, r, ks)
                v_aug = jnp.concatenate([v, jnp.ones_like(v)], axis=1)
                pv2 = jnp.dot(p_all[idx, :, 0:tk], v_aug, preferred_element_type=F32)
                pv = jnp.where(first_head_out, pv2[:TQ, :LANES], pv2[TQ:, :LANES])
                dd = jnp.where(first_head_out, pv2[:TQ, LANES:], pv2[TQ:, LANES:])
                mm = m_all[idx]
                if dilation == 1:
                    rows = pl.ds(_aligned(lw, TQ), TQ)
                else:
                    rows = pl.ds(lw * dilation + r, TQ, stride=dilation)
                if g == 0:
                    acc_ref[rows, :] = pv
                    m_ref[rows, :] = mm
                    l_ref[rows, :] = dd
                else:
                    m_old = m_ref[rows, :]
                    m_new = jnp.maximum(m_old, mm)
                    e_old = jnp.exp(m_old - m_new)
                    e_new = jnp.exp(mm - m_new)
                    acc_ref[rows, :] = acc_ref[rows, :] * e_old + pv * e_new
                    l_ref[rows, :] = l_ref[rows, :] * e_old + dd * e_new
                    if g + 1 < len(groups):
                        m_ref[rows, :] = m_new

            for phase in (scores, softmax, values):
                for idx in range(n_blocks):
                    phase(idx)
        gate = _silu(z_ref[pl.ds(base, TILE), :].astype(F32))
        y_ref[pl.ds(base, TILE), :] = (acc_ref[...] / l_ref[...] * gate).astype(BF16)
        return carry

    if n_windows == 1:
        window(0, 0)
    else:
        lax.fori_loop(0, n_windows, window, 0)


def _fused_attention(qkvz, batch, seq_len):
    view = qkvz.reshape(N_ATTN_SLABS, N_HEAD_PAIRS, batch, seq_len, LANES)
    slab_bytes = seq_len * LANES * 2
    mode = {} if 2 * N_ATTN_SLABS * slab_bytes <= VMEM_LIMIT // 2 else {"pipeline_mode": pl.Buffered(1)}

    def slab_spec(slab):
        return pl.BlockSpec((None, None, None, seq_len, LANES), lambda b, hp: (slab, hp, b, 0, 0), **mode)

    order = [ATTN_SLABS["q"][0], ATTN_SLABS["k"][0], ATTN_SLABS["v"][0], ATTN_SLABS["z"],
             ATTN_SLABS["q"][1], ATTN_SLABS["k"][1], ATTN_SLABS["v"][1],
             ATTN_SLABS["q"][2], ATTN_SLABS["k"][2], ATTN_SLABS["v"][2]]
    tk_max = min(4 * 64, seq_len // GROUP_PATTERNS[0][1])
    y = pl.pallas_call(
        functools.partial(_fused_attn_kernel, seq_len=seq_len),
        grid=(batch, N_HEAD_PAIRS),
        in_specs=[slab_spec(s) for s in order],
        out_specs=pl.BlockSpec((None, None, seq_len, LANES), lambda b, hp: (hp, b, 0, 0)),
        out_shape=jax.ShapeDtypeStruct((N_HEAD_PAIRS, batch, seq_len, LANES), BF16),
        scratch_shapes=[pltpu.VMEM((TILE, LANES), F32)] * 3 + [
            pltpu.VMEM((3, 2 * TQ, tk_max), F32),
            pltpu.VMEM((TILE // TQ, 2 * TQ, tk_max), F32),
            pltpu.VMEM((TILE // TQ, 2 * TQ, tk_max), BF16),
            pltpu.VMEM((TILE // TQ, TQ, LANES), F32)],
        compiler_params=_params("parallel", "parallel"),
        name="fused_banded_attn",
    )(*([view] * N_ATTN_SLABS))
    return y.reshape(N_HEAD_PAIRS, batch * seq_len, LANES)


def _attn_out_kernel(y_ref, x_ref, w_ref, out_ref):
    y = jnp.concatenate([y_ref[hp] for hp in range(N_HEAD_PAIRS)], axis=1)
    out_ref[...] = x_ref[...] + jnp.dot(y, w_ref[...], preferred_element_type=F32)


def _attn_out(y, x2d, w_out):
    ntok = x2d.shape[0]
    tm = 512
    return pl.pallas_call(
        _attn_out_kernel,
        grid=(ntok // tm,),
        in_specs=[
            pl.BlockSpec((N_HEAD_PAIRS, tm, LANES), lambda i: (0, i, 0)),
            pl.BlockSpec((tm, D_MODEL), lambda i: (i, 0)),
            pl.BlockSpec((D_MODEL, D_MODEL), lambda i: (0, 0)),
        ],
        out_specs=pl.BlockSpec((tm, D_MODEL), lambda i: (i, 0)),
        out_shape=jax.ShapeDtypeStruct((ntok, D_MODEL), F32),
        compiler_params=_params("parallel"),
        name="attn_outproj",
    )(y, x2d, w_out)


def _ssm_inproj_kernel(x_ref, g_ref, wt_ref, o_ref, hn_ref, xs_ref):
    @pl.when(pl.program_id(1) == 0)
    def _():
        _stage_lane_blocks(x_ref[...], xs_ref)
        gain = g_ref[...]
        for t in range(CHUNK):
            rows = _strided_rows(xs_ref, t, LANES, CHUNK)
            hn_ref[t * LANES:(t + 1) * LANES, :] = _rmsnorm_rows(rows, gain).astype(BF16)

    half = TILE // 2
    for part in range(2):
        cols = slice(part * half, (part + 1) * half)
        o_ref[:, cols] = lax.dot_general(wt_ref[...], hn_ref[cols, :], (((1,), (1,)), ((), ())),
                                         preferred_element_type=F32).astype(BF16)


def _ssm_inproj(x2d, gain, w_in_t):
    ntok = x2d.shape[0]
    return pl.pallas_call(
        _ssm_inproj_kernel,
        grid=(ntok // TILE, 2),
        in_specs=[
            pl.BlockSpec((TILE, D_MODEL), lambda i, j: (i, 0)),
            pl.BlockSpec((1, D_MODEL), lambda i, j: (0, 0)),
            pl.BlockSpec((D_MODEL, D_MODEL), lambda i, j: (j, 0)),
        ],
        out_specs=pl.BlockSpec((None, D_MODEL, TILE), lambda i, j: (j, 0, i)),
        out_shape=jax.ShapeDtypeStruct((2, D_MODEL, ntok), BF16),
        scratch_shapes=[pltpu.VMEM((TILE, D_MODEL), BF16), pltpu.VMEM((LANE_BLOCKS, TILE, LANES), F32)],
        compiler_params=_params("parallel", "arbitrary"),
        name="ssm_inproj",
    )(x2d, gain, w_in_t)


def _ssm_core_kernel(u_ref, mt_ref, w_ref, vt_ref, a16_ref, apl_ref, y_ref, ut_ref, s_ref, hf_ref, hb_ref,
                     *, n_seq, chunks_per_seq):
    n_tiles = n_seq * chunks_per_seq // LANES
    p = SSM_STATE
    sub = 8
    for tile in range(n_tiles):
        for t in range(CHUNK):
            src = (tile * CHUNK + t) * LANES
            ut_ref[t * SSM_GROUP:(t + 1) * SSM_GROUP, tile * LANES:(tile + 1) * LANES] = u_ref[:, src:src + LANES]
    ut = ut_ref[...]
    s_ref[...] = lax.dot_general(ut, w_ref[...], (((0,), (0,)), ((), ())), preferred_element_type=F32)

    lane = lax.broadcasted_iota(jnp.int32, (sub, LANES), 1)
    row = lax.broadcasted_iota(jnp.int32, (sub, LANES), 0)
    is_f = lane < p
    steps_in = jnp.where(is_f, row, sub - 1 - row)
    cur = (jnp.broadcast_to(a16_ref[0:1, :], (sub, LANES)), jnp.broadcast_to(a16_ref[1:2, :], (sub, LANES)))
    levels = []
    shift = 1
    while shift < sub:
        levels.append((shift, steps_in >= shift, cur))
        cur = _cmul(cur[0], cur[1], cur[0], cur[1])
        shift *= 2
    apl_re = apl_ref[0]
    apl_im = apl_ref[1]
    inner = steps_in >= 1
    n_steps = chunks_per_seq // sub

    def toward_scan_start(x, by):
        return jnp.where(is_f, pltpu.roll(x, by, axis=0), pltpu.roll(x, sub - by, axis=0))

    def seq_body(b, carry):
        seq0 = b * chunks_per_seq

        def step(i, h):
            h_re, h_im = h
            rf = pl.multiple_of(seq0 + i * sub, sub)
            rb = pl.multiple_of(seq0 + (n_steps - 1 - i) * sub, sub)
            x_re = jnp.where(is_f, s_ref[pl.ds(rf, sub), 0:LANES], s_ref[pl.ds(rb, sub), 0:LANES])
            x_im = jnp.where(is_f, s_ref[pl.ds(rf, sub), LANES:2 * LANES], s_ref[pl.ds(rb, sub), LANES:2 * LANES])
            for by, keep, (lr, li) in levels:
                d_re, d_im = _cmul(lr, li, jnp.where(keep, toward_scan_start(x_re, by), 0.0),
                                   jnp.where(keep, toward_scan_start(x_im, by), 0.0))
                x_re = x_re + d_re
                x_im = x_im + d_im
            c_re, c_im = _cmul(apl_re, apl_im, h_re, h_im)
            t_re = x_re + c_re
            t_im = x_im + c_im
            h0_re = jnp.where(inner, toward_scan_start(t_re, 1), h_re)
            h0_im = jnp.where(inner, toward_scan_start(t_im, 1), h_im)
            hf_ref[pl.ds(rf, sub), 0:LANES] = h0_re
            hf_ref[pl.ds(rf, sub), LANES:2 * LANES] = h0_im
            hb_ref[pl.ds(rb, sub), 0:LANES] = h0_re
            hb_ref[pl.ds(rb, sub), LANES:2 * LANES] = h0_im
            n_re = jnp.where(is_f[0:1], t_re[sub - 1:sub], t_re[0:1])
            n_im = jnp.where(is_f[0:1], t_im[sub - 1:sub], t_im[0:1])
            return n_re, n_im

        zero = jnp.zeros((1, LANES), F32)
        lax.fori_loop(0, n_steps, step, (zero, zero), unroll=4)
        return carry

    lax.fori_loop(0, n_seq, seq_body, 0)
    is_f_wide = lax.rem(lax.broadcasted_iota(jnp.int32, (1, 2 * LANES), 1), LANES) < p
    h0 = jnp.where(is_f_wide, hf_ref[...], hb_ref[...]).astype(BF16)
    yt = (jnp.dot(mt_ref[...], ut, preferred_element_type=F32)
          + lax.dot_general(vt_ref[...], h0, (((1,), (1,)), ((), ())), preferred_element_type=F32)).astype(BF16)
    for tile in range(n_tiles):
        for t in range(CHUNK):
            dst = (tile * CHUNK + t) * LANES
            y_ref[:, dst:dst + LANES] = yt[t * SSM_GROUP:(t + 1) * SSM_GROUP, tile * LANES:(tile + 1) * LANES]


def _ssm_core(uz_t, mats, batch, seq_len):
    m_t, w_n, v_t, a16, apl = mats
    ntok = batch * seq_len
    n_chunks = ntok // CHUNK
    mat_spec = pl.BlockSpec((None, CHUNK_COLS, CHUNK_COLS), lambda g: (g, 0, 0))
    return pl.pallas_call(
        functools.partial(_ssm_core_kernel, n_seq=batch, chunks_per_seq=seq_len // CHUNK),
        grid=(SSM_GROUPS,),
        in_specs=[pl.BlockSpec((None, SSM_GROUP, ntok), lambda g: (0, g, 0)),
                  mat_spec, mat_spec, mat_spec,
                  pl.BlockSpec((None, 2, LANES), lambda g: (g, 0, 0)),
                  pl.BlockSpec((None, 2, 8, LANES), lambda g: (g, 0, 0, 0))],
        out_specs=pl.BlockSpec((SSM_GROUP, ntok), lambda g: (g, 0)),
        out_shape=jax.ShapeDtypeStruct((D_MODEL, ntok), BF16),
        scratch_shapes=[pltpu.VMEM((CHUNK_COLS, n_chunks), BF16)] + [pltpu.VMEM((n_chunks, 2 * LANES), F32)] * 3,
        compiler_params=_params("parallel"),
        name="ssm_chunk_scan",
    )(uz_t, m_t, w_n, v_t, a16, apl)


def _ssm_glu_kernel(y_ref, z_ref, wg_ref, bg_ref, o_ref):
    gl = _gelu_tanh(y_ref[...].astype(F32))
    t = jnp.dot(wg_ref[...], gl.astype(BF16), preferred_element_type=F32) + bg_ref[...]
    y = gl * _sigmoid(t)
    o_ref[...] = (y * _silu(z_ref[...].astype(F32))).astype(BF16)


def _ssm_glu(y_t, uz_t, w_glu_t, b_glu_col):
    ntok = y_t.shape[1]
    tl = 1024
    return pl.pallas_call(
        _ssm_glu_kernel,
        grid=(ntok // tl,),
        in_specs=[pl.BlockSpec((D_MODEL, tl), lambda i: (0, i)),
                  pl.BlockSpec((None, D_MODEL, tl), lambda i: (1, 0, i)),
                  pl.BlockSpec((D_MODEL, D_MODEL), lambda i: (0, 0)),
                  pl.BlockSpec((D_MODEL, 1), lambda i: (0, 0))],
        out_specs=pl.BlockSpec((D_MODEL, tl), lambda i: (0, i)),
        out_shape=jax.ShapeDtypeStruct((D_MODEL, ntok), BF16),
        compiler_params=_params("parallel"),
        name="ssm_glu",
    )(y_t, uz_t, w_glu_t, b_glu_col)


def _ssm_out_kernel(y_ref, x_ref, w_ref, fg_ref, out_ref, ys_ref, *, final):
    half = TILE // 2
    for part in range(2):
        res = lax.dot_general(y_ref[:, part * half:(part + 1) * half], w_ref[...], (((0,), (0,)), ((), ())),
                              preferred_element_type=F32)
        for tt in range(CHUNK // 2):
            rows = pl.ds(part * (CHUNK // 2) + tt, LANES, stride=CHUNK)
            for c in range(LANE_BLOCKS):
                ys_ref[c, rows, :] = res[tt * LANES:(tt + 1) * LANES, c * LANES:(c + 1) * LANES]
    out = x_ref[...] + jnp.concatenate([ys_ref[c] for c in range(LANE_BLOCKS)], axis=1)
    if final:
        out = _rmsnorm_rows(out, fg_ref[...])
    out_ref[...] = out


def _ssm_out(y3_t, x2d, w_out, final_gain, final):
    ntok = x2d.shape[0]
    return pl.pallas_call(
        functools.partial(_ssm_out_kernel, final=final),
        grid=(ntok // TILE,),
        in_specs=[pl.BlockSpec((D_MODEL, TILE), lambda i: (0, i)),
                  pl.BlockSpec((TILE, D_MODEL), lambda i: (i, 0), pipeline_mode=pl.Buffered(1)),
                  pl.BlockSpec((D_MODEL, D_MODEL), lambda i: (0, 0)),
                  pl.BlockSpec((1, D_MODEL), lambda i: (0, 0))],
        out_specs=pl.BlockSpec((TILE, D_MODEL), lambda i: (i, 0)),
        out_shape=jax.ShapeDtypeStruct((ntok, D_MODEL), F32),
        scratch_shapes=[pltpu.VMEM((LANE_BLOCKS, TILE, LANES), F32)],
        compiler_params=_params("parallel"),
        name="ssm_outproj",
    )(y3_t, x2d, w_out, final_gain)


def _pair_heads(w):
    rows = w.shape[0]
    return w.reshape(rows, N_HEAD_PAIRS, 2, 2, HEAD_DIM // 2).transpose(0, 1, 3, 2, 4).reshape(rows, -1)


def _prep_attn_w_in(w_in):
    width = N_HEADS * HEAD_DIM
    blocks = [None] * N_ATTN_SLABS
    for g in range(len(GROUP_PATTERNS)):
        base = 3 * g * width
        blocks[ATTN_SLABS["q"][g]] = _pair_heads(w_in[:, base:base + width]) * (HEAD_DIM ** -0.5)
        blocks[ATTN_SLABS["k"][g]] = _pair_heads(w_in[:, base + width:base + 2 * width])
        blocks[ATTN_SLABS["v"][g]] = w_in[:, base + 2 * width:base + 3 * width]
    blocks[ATTN_SLABS["z"]] = w_in[:, 3 * len(GROUP_PATTERNS) * width:]
    return jnp.concatenate(blocks, axis=1).astype(BF16)


def _rope_tables(seq_len):
    inv_freq = ROPE_THETA ** (-jnp.arange(0, HEAD_DIM, 2, dtype=F32) / HEAD_DIM)
    ang = jnp.arange(seq_len, dtype=F32)[:, None] * inv_freq[None, :]
    cos, sin = jnp.cos(ang), jnp.sin(ang)
    cos_t = jnp.concatenate([cos] * 4, axis=1)
    sin_t = jnp.concatenate([-sin, -sin, sin, sin], axis=1)

    def orders(tab, identity):
        out = []
        for _, dilation in GROUP_PATTERNS:
            rpc = TILE // dilation
            out.append(tab.reshape(seq_len // TILE, rpc, dilation, LANES).transpose(0, 2, 1, 3).reshape(seq_len, LANES))
        out.append(jnp.full_like(tab, identity))
        return jnp.stack(out)

    return orders(cos_t, 1.0), orders(sin_t, 0.0)


def _prep_ssm(a_re, a_im, log_step, b_re, b_im, c_re, c_im, d):
    hi = lax.Precision.HIGHEST
    g_n, p_n, c_n, t_n = SSM_GROUPS, SSM_STATE, SSM_GROUP, CHUNK
    dt = jnp.exp(log_step)[..., None]
    ldr = a_re * dt
    ldi = a_im * dt
    abar_re = jnp.exp(ldr) * jnp.cos(ldi)
    abar_im = jnp.exp(ldr) * jnp.sin(ldi)
    den = a_re * a_re + a_im * a_im
    nr = abar_re - 1.0
    ni = abar_im
    f_re = (nr * a_re + ni * a_im) / den
    f_im = (ni * a_re - nr * a_im) / den
    bb_re = f_re[..., None] * b_re - f_im[..., None] * b_im
    bb_im = f_re[..., None] * b_im + f_im[..., None] * b_re
    n = jnp.arange(t_n + 1, dtype=F32)[:, None, None, None]
    pw_re = jnp.exp(n * ldr) * jnp.cos(n * ldi)
    pw_im = jnp.exp(n * ldr) * jnp.sin(n * ldi)
    ca_re = c_re[None] * pw_re[:, :, :, None, :] - c_im[None] * pw_im[:, :, :, None, :]
    ca_im = c_re[None] * pw_im[:, :, :, None, :] + c_im[None] * pw_re[:, :, :, None, :]
    kern = jnp.einsum('ldgcp,dgpk->ldgck', jnp.concatenate([ca_re, -ca_im], axis=-1),
                      jnp.concatenate([bb_re, bb_im], axis=2), precision=hi)
    s_idx = jnp.arange(t_n)[:, None]
    t_idx = jnp.arange(t_n)[None, :]
    lag = t_idx - s_idx
    m_f = kern[jnp.clip(lag, 0, t_n - 1), 0] * (lag >= 0)[:, :, None, None, None].astype(F32)
    m_b = kern[jnp.clip(-lag, 0, t_n - 1), 1] * (lag <= 0)[:, :, None, None, None].astype(F32)
    m_mat = (m_f + m_b).transpose(2, 0, 4, 1, 3).reshape(g_n, CHUNK_COLS, CHUNK_COLS)
    skip = jnp.tile(d.reshape(g_n, c_n), (1, t_n))
    m_mat = m_mat + jnp.eye(CHUNK_COLS, dtype=F32)[None] * skip[:, None, :]

    def state_in(pw_r, pw_i, direction):
        re = pw_r[..., None] * bb_re[direction][None] - pw_i[..., None] * bb_im[direction][None]
        im = pw_r[..., None] * bb_im[direction][None] + pw_i[..., None] * bb_re[direction][None]
        to_cols = lambda x: x.transpose(1, 0, 3, 2).reshape(g_n, CHUNK_COLS, p_n)
        return to_cols(re), to_cols(im)

    rev = t_n - 1 - jnp.arange(t_n)
    wf_re, wf_im = state_in(pw_re[rev, 0], pw_im[rev, 0], 0)
    wb_re, wb_im = state_in(pw_re[:t_n, 1], pw_im[:t_n, 1], 1)
    w_mat = jnp.concatenate([wf_re, wb_re, wf_im, wb_im], axis=2)

    def state_out(ca_r, ca_i):
        to_rows = lambda x: x.transpose(1, 3, 0, 2).reshape(g_n, p_n, CHUNK_COLS)
        return to_rows(ca_r), to_rows(-ca_i)

    vf_re, vf_im = state_out(ca_re[1:t_n + 1, 0], ca_im[1:t_n + 1, 0])
    back = t_n - jnp.arange(t_n)
    vb_re, vb_im = state_out(ca_re[back, 1], ca_im[back, 1])
    v_mat = jnp.concatenate([vf_re, vb_re, vf_im, vb_im], axis=1)

    lanes = lambda f, b: jnp.concatenate([f, b], axis=-1)
    a16 = jnp.stack([lanes(pw_re[t_n, 0], pw_re[t_n, 1]), lanes(pw_im[t_n, 0], pw_im[t_n, 1])], axis=1)
    j_row = jnp.arange(8, dtype=F32)[None, :, None]
    n_f = t_n * (j_row + 1.0)
    n_b = t_n * (8.0 - j_row)
    pow_rows = lambda nn, dr: (jnp.exp(nn * ldr[dr][:, None, :]) * jnp.cos(nn * ldi[dr][:, None, :]),
                               jnp.exp(nn * ldr[dr][:, None, :]) * jnp.sin(nn * ldi[dr][:, None, :]))
    plf_re, plf_im = pow_rows(n_f, 0)
    plb_re, plb_im = pow_rows(n_b, 1)
    apl = jnp.stack([lanes(plf_re, plb_re), lanes(plf_im, plb_im)], axis=1)
    tr = lambda x: x.transpose(0, 2, 1).astype(BF16)
    return tr(m_mat), w_mat.astype(BF16), tr(v_mat), a16, apl


def _attention_layer(x2d, batch, seq_len, params, rope):
    gain, w_in_p, w_out = params
    qkvz = _attn_inproj(x2d, gain, w_in_p, rope[0], rope[1], seq_len)
    y = _fused_attention(qkvz, batch, seq_len)
    return _attn_out(y, x2d, w_out)


def _ssm_layer(x2d, batch, seq_len, params, final_gain, final):
    gain, w_in_t, mats, w_glu_t, b_glu_col, w_out = params
    uz_t = _ssm_inproj(x2d, gain, w_in_t)
    y_t = _ssm_core(uz_t, mats, batch, seq_len)
    y3_t = _ssm_glu(y_t, uz_t, w_glu_t, b_glu_col)
    return _ssm_out(y3_t, x2d, w_out, final_gain, final)


def _trunk(x, attn_params, ssm_params, final_gain):
    batch, seq_len, _ = x.shape
    assert seq_len % TILE == 0
    rope = _rope_tables(seq_len)
    x2d = x.reshape(batch * seq_len, D_MODEL)
    for i in range(DEPTH):
        j = i // 2
        if i % 2 == 0:
            x2d = _attention_layer(x2d, batch, seq_len, attn_params[j], rope)
        else:
            x2d = _ssm_layer(x2d, batch, seq_len, ssm_params[j], final_gain, final=(i == DEPTH - 1))
    return x2d.reshape(batch, seq_len, D_MODEL)


def _prepare(attn_norm, attn_w_in, attn_w_out, ssm_norm, ssm_w_in, ssm_a_re, ssm_a_im, ssm_log_step, ssm_b_re,
             ssm_b_im, ssm_c_re, ssm_c_im, ssm_d, ssm_w_glu, ssm_b_glu, ssm_w_out, final_norm):
    attn_params = []
    for j in range(attn_norm.shape[0]):
        attn_params.append((attn_norm[j][None, :], _prep_attn_w_in(attn_w_in[j]), attn_w_out[j].astype(BF16)))
    ssm_params = []
    for j in range(ssm_norm.shape[0]):
        mats = _prep_ssm(ssm_a_re[j], ssm_a_im[j], ssm_log_step[j], ssm_b_re[j], ssm_b_im[j],
                         ssm_c_re[j], ssm_c_im[j], ssm_d[j])
        ssm_params.append((ssm_norm[j][None, :], ssm_w_in[j].T.astype(BF16), mats,
                           ssm_w_glu[j].T.astype(BF16), ssm_b_glu[j][:, None], ssm_w_out[j].astype(BF16)))
    return attn_params, ssm_params, final_norm[None, :]


def kernel(x_prompt, x_sample, attn_norm, attn_w_in, attn_w_out, ssm_norm, ssm_w_in, ssm_a_re, ssm_a_im, ssm_log_step, ssm_b_re, ssm_b_im, ssm_c_re, ssm_c_im, ssm_d, ssm_w_glu, ssm_b_glu, ssm_w_out, final_norm):
    attn_params, ssm_params, final_gain = _prepare(
        attn_norm, attn_w_in, attn_w_out, ssm_norm, ssm_w_in, ssm_a_re, ssm_a_im, ssm_log_step, ssm_b_re,
        ssm_b_im, ssm_c_re, ssm_c_im, ssm_d, ssm_w_glu, ssm_b_glu, ssm_w_out, final_norm)
    y_prompt = _trunk(x_prompt, attn_params, ssm_params, final_gain)
    y_sample = _trunk(x_sample, attn_params, ssm_params, final_gain)
    return (y_prompt, y_sample)
```

```python
import functools
import math

import jax
import jax.numpy as jnp
from jax import lax
from jax.experimental import pallas as pl
from jax.experimental.pallas import tpu as pltpu

F32 = jnp.float32
BF16 = jnp.bfloat16

D_MODEL = 1024
DEPTH = 4
GROUP_PATTERNS = ((128, 1), (512, 4), (2048, 16))
N_HEADS = 16
HEAD_DIM = 64
N_HEAD_PAIRS = N_HEADS // 2
LANES = 128
LANE_BLOCKS = D_MODEL // LANES
ROPE_THETA = 10000.0
SSM_GROUP = 16
SSM_GROUPS = D_MODEL // SSM_GROUP
SSM_STATE = 64
CHUNK = 16
CHUNK_COLS = CHUNK * SSM_GROUP
TILE = CHUNK * LANES
TQ = 128
NORM_EPS = 1e-6
VMEM_LIMIT = 56 * 1024 * 1024

ATTN_SLABS = {"q": (0, 4, 7), "k": (1, 5, 8), "v": (2, 6, 9), "z": 3}
N_ATTN_SLABS = 10


def _sigmoid(x):
    return 1.0 / (1.0 + jnp.exp(-x))


def _silu(x):
    return x * _sigmoid(x)


def _gelu_tanh(x):
    c = math.sqrt(2.0 / math.pi)
    return x * (0.5 * (1.0 + jnp.tanh(c * (x + 0.044715 * (x * x * x)))))


def _rmsnorm_rows(xf, gain):
    return xf * lax.rsqrt(jnp.mean(xf * xf, axis=-1, keepdims=True) + NORM_EPS) * gain


def _aligned(x, multiple):
    return x if isinstance(x, int) else pl.multiple_of(x, multiple)


def _clip(x, lo, hi):
    return min(max(x, lo), hi) if isinstance(x, int) else jnp.clip(x, lo, hi)


def _cmul(ar, ai, br, bi):
    return ar * br - ai * bi, ar * bi + ai * br


def _params(*semantics):
    return pltpu.CompilerParams(dimension_semantics=semantics, vmem_limit_bytes=VMEM_LIMIT)


def _stage_lane_blocks(x, slab_ref):
    for c in range(LANE_BLOCKS):
        slab_ref[c] = x[:, c * LANES:(c + 1) * LANES]


def _strided_rows(slab_ref, start, count, stride):
    return jnp.concatenate(
        [slab_ref[c, pl.ds(start, count, stride=stride), :] for c in range(LANE_BLOCKS)], axis=1)


def _attn_inproj_kernel(x_ref, g_ref, w_ref, cos_ref, sin_ref, o_ref, hn_ref, xs_ref):
    j = pl.program_id(1)

    @pl.when(j == 0)
    def _():
        hn = _rmsnorm_rows(x_ref[...], g_ref[...])
        _stage_lane_blocks(hn, xs_ref)
        hn_ref[...] = hn.astype(BF16)

    for g in (1, 2):
        @pl.when(j == ATTN_SLABS["q"][g])
        def _(dilation=GROUP_PATTERNS[g][1]):
            rpc = TILE // dilation
            for r in range(dilation):
                hn_ref[r * rpc:(r + 1) * rpc, :] = _strided_rows(xs_ref, r, rpc, dilation).astype(BF16)

    half = TILE // 2
    for part in range(2):
        rows = slice(part * half, (part + 1) * half)
        acc = jnp.dot(hn_ref[rows, :], w_ref[...], preferred_element_type=F32)
        c = cos_ref[rows, :]
        s = sin_ref[rows, :]
        for hp in range(N_HEAD_PAIRS):
            t = acc[:, hp * LANES:(hp + 1) * LANES]
            o_ref[hp, rows, :] = (t * c + pltpu.roll(t, 64, axis=1) * s).astype(BF16)


def _attn_inproj(x2d, gain, w_p, cos_t, sin_t, seq_len):
    ntok = x2d.shape[0]
    assert seq_len % TILE == 0
    pos_blocks = seq_len // TILE

    def table_map(i, j):
        order = jnp.where(j >= ATTN_SLABS["q"][2], 2, jnp.where(j >= ATTN_SLABS["q"][1], 1, 0))
        is_plain = functools.reduce(jnp.logical_or, [j == b for b in ATTN_SLABS["v"] + (ATTN_SLABS["z"],)])
        return (jnp.where(is_plain, len(GROUP_PATTERNS), order), i % pos_blocks, 0)

    return pl.pallas_call(
        _attn_inproj_kernel,
        grid=(ntok // TILE, N_ATTN_SLABS),
        in_specs=[
            pl.BlockSpec((TILE, D_MODEL), lambda i, j: (i, 0)),
            pl.BlockSpec((1, D_MODEL), lambda i, j: (0, 0)),
            pl.BlockSpec((D_MODEL, D_MODEL), lambda i, j: (0, j)),
            pl.BlockSpec((None, TILE, LANES), table_map),
            pl.BlockSpec((None, TILE, LANES), table_map),
        ],
        out_specs=pl.BlockSpec((None, N_HEAD_PAIRS, TILE, LANES), lambda i, j: (j, 0, i, 0)),
        out_shape=jax.ShapeDtypeStruct((N_ATTN_SLABS, N_HEAD_PAIRS, ntok, LANES), BF16),
        scratch_shapes=[pltpu.VMEM((TILE, D_MODEL), BF16), pltpu.VMEM((LANE_BLOCKS, TILE, LANES), F32)],
        compiler_params=_params("parallel", "arbitrary"),
        name="attn_inproj",
    )(x2d, gain, w_p, cos_t, sin_t)


def _fused_attn_kernel(q0, k0, v0, z_ref, q1, k1, v1, q2, k2, v2, y_ref, acc_ref, m_ref, l_ref, bias_ref,
                       p_all, m_all, *, seq_len):
    lane = lax.broadcasted_iota(jnp.int32, (1, LANES), 1)
    head_of_lane = lax.rem(lane, 64) // 32
    first_head_out = lane < HEAD_DIM
    groups = ((q0, k0, v0), (q1, k1, v1), (q2, k2, v2))

    tk_max = bias_ref.shape[2]
    row_i = lax.broadcasted_iota(jnp.int32, (TQ, tk_max), 0)
    col_i = lax.broadcasted_iota(jnp.int32, (TQ, tk_max), 1)
    for o in range(bias_ref.shape[0]):
        bias_ref[o] = jnp.where(jnp.abs(row_i - col_i + o * 64) <= 64, 0.0, -jnp.inf).astype(F32)

    n_windows = seq_len // TILE

    def window(w, carry):
        base = _aligned(w * TILE, TILE)
        for g, (q_ref, k_ref, v_ref) in enumerate(groups):
            win, dilation = GROUP_PATTERNS[g]
            half = win // (2 * dilation)
            length = seq_len // dilation
            rpc = TILE // dilation
            nqb = rpc // TQ
            tk = min(4 * half, length)
            n_pieces = tk // half
            n_blocks = dilation * nqb

            def locate(idx, dilation=dilation, half=half, length=length, rpc=rpc, nqb=nqb, tk=tk):
                r = idx // nqb
                lw = (idx % nqb) * TQ
                l0 = w * rpc + lw
                ks = _clip(l0 - half, 0, length - tk)
                return r, lw, l0, ks

            def key_rows(ref, r, ks, rpc=rpc, half=half, n_pieces=n_pieces):
                parts = []
                for p in range(n_pieces):
                    l = ks + p * half
                    row = _aligned((l // rpc) * TILE + r * rpc + l % rpc, half)
                    parts.append(ref[pl.ds(row, half), :])
                return jnp.concatenate(parts, axis=0)

            def probabilities(idx, q_ref=q_ref, k_ref=k_ref, rpc=rpc, half=half, tk=tk):
                r, lw, l0, ks = locate(idx)
                q = q_ref[pl.ds(_aligned(base + r * rpc + lw, TQ), TQ), :]
                k = key_rows(k_ref, r, ks)
                bias = bias_ref[(l0 - ks) // half][:, 0:tk]
                maxima = []
                for h in range(2):
                    q_h = jnp.where(head_of_lane == h, q, jnp.zeros_like(q))
                    s = lax.dot_general(q_h, k, (((1,), (1,)), ((), ())), preferred_element_type=F32) + bias
                    m = jnp.max(s, axis=-1, keepdims=True)
                    p_all[idx, h * TQ:(h + 1) * TQ, 0:tk] = jnp.exp(s - m).astype(BF16)
                    maxima.append(m)
                m_all[idx] = jnp.where(first_head_out, maxima[0], maxima[1])

            def values(idx, g=g, v_ref=v_ref, dilation=dilation, tk=tk):
                r, lw, l0, ks = locate(idx)
                v = key_rows(v_ref, r, ks)
                v_aug = jnp.concatenate([v, jnp.ones_like(v)], axis=1)
                pv2 = jnp.dot(p_all[idx, :, 0:tk], v_aug, preferred_element_type=F32)
                pv = jnp.where(first_head_out, pv2[:TQ, :LANES], pv2[TQ:, :LANES])
                dd = jnp.where(first_head_out, pv2[:TQ, LANES:], pv2[TQ:, LANES:])
                mm = m_all[idx]
                if dilation == 1:
                    rows = pl.ds(_aligned(lw, TQ), TQ)
                else:
                    rows = pl.ds(lw * dilation + r, TQ, stride=dilation)
                if g == 0:
                    acc_ref[rows, :] = pv
                    m_ref[rows, :] = mm
                    l_ref[rows, :] = dd
                else:
                    m_old = m_ref[rows, :]
                    m_new = jnp.maximum(m_old, mm)
                    e_old = jnp.exp(m_old - m_new)
                    e_new = jnp.exp(mm - m_new)
                    acc_ref[rows, :] = acc_ref[rows, :] * e_old + pv * e_new
                    l_ref[rows, :] = l_ref[rows, :] * e_old + dd * e_new
                    if g + 1 < len(groups):
                        m_ref[rows, :] = m_new

            for phase in (probabilities, values):
                for idx in range(n_blocks):
                    phase(idx)
        gate = _silu(z_ref[pl.ds(base, TILE), :].astype(F32))
        y_ref[pl.ds(base, TILE), :] = (acc_ref[...] / l_ref[...] * gate).astype(BF16)
        return carry

    if n_windows == 1:
        window(0, 0)
    else:
        lax.fori_loop(0, n_windows, window, 0)


def _fused_attention(qkvz, batch, seq_len):
    view = qkvz.reshape(N_ATTN_SLABS, N_HEAD_PAIRS, batch, seq_len, LANES)
    slab_bytes = seq_len * LANES * 2
    scratch_bytes = (3 * TILE * LANES + 3 * TQ * 4 * 64 + TILE * LANES) * 4 + TILE * 2 * 4 * 64 * 2
    double_buffered = 2 * (N_ATTN_SLABS + 1) * slab_bytes + scratch_bytes
    mode = {} if double_buffered <= VMEM_LIMIT - 4 * 1024 * 1024 else {"pipeline_mode": pl.Buffered(1)}

    def slab_spec(slab):
        return pl.BlockSpec((None, None, None, seq_len, LANES), lambda b, hp: (slab, hp, b, 0, 0), **mode)

    order = [ATTN_SLABS["q"][0], ATTN_SLABS["k"][0], ATTN_SLABS["v"][0], ATTN_SLABS["z"],
             ATTN_SLABS["q"][1], ATTN_SLABS["k"][1], ATTN_SLABS["v"][1],
             ATTN_SLABS["q"][2], ATTN_SLABS["k"][2], ATTN_SLABS["v"][2]]
    tk_max = min(4 * 64, seq_len // GROUP_PATTERNS[0][1])
    y = pl.pallas_call(
        functools.partial(_fused_attn_kernel, seq_len=seq_len),
        grid=(batch, N_HEAD_PAIRS),
        in_specs=[slab_spec(s) for s in order],
        out_specs=pl.BlockSpec((None, None, seq_len, LANES), lambda b, hp: (hp, b, 0, 0)),
        out_shape=jax.ShapeDtypeStruct((N_HEAD_PAIRS, batch, seq_len, LANES), BF16),
        scratch_shapes=[pltpu.VMEM((TILE, LANES), F32)] * 3 + [
            pltpu.VMEM((3, TQ, tk_max), F32),
            pltpu.VMEM((TILE // TQ, 2 * TQ, tk_max), BF16),
            pltpu.VMEM((TILE // TQ, TQ, LANES), F32)],
        compiler_params=_params("parallel", "parallel"),
        name="fused_banded_attn",
    )(*([view] * N_ATTN_SLABS))
    return y.reshape(N_HEAD_PAIRS, batch * seq_len, LANES)


def _attn_out_kernel(y_ref, x_ref, w_ref, out_ref):
    y = jnp.concatenate([y_ref[hp] for hp in range(N_HEAD_PAIRS)], axis=1)
    out_ref[...] = x_ref[...] + jnp.dot(y, w_ref[...], preferred_element_type=F32)


def _attn_out(y, x2d, w_out):
    ntok = x2d.shape[0]
    tm = 512
    return pl.pallas_call(
        _attn_out_kernel,
        grid=(ntok // tm,),
        in_specs=[
            pl.BlockSpec((N_HEAD_PAIRS, tm, LANES), lambda i: (0, i, 0)),
            pl.BlockSpec((tm, D_MODEL), lambda i: (i, 0)),
            pl.BlockSpec((D_MODEL, D_MODEL), lambda i: (0, 0)),
        ],
        out_specs=pl.BlockSpec((tm, D_MODEL), lambda i: (i, 0)),
        out_shape=jax.ShapeDtypeStruct((ntok, D_MODEL), F32),
        compiler_params=_params("parallel"),
        name="attn_outproj",
    )(y, x2d, w_out)


def _ssm_inproj_kernel(x_ref, g_ref, wt_ref, o_ref, hn_ref, xs_ref):
    @pl.when(pl.program_id(1) == 0)
    def _():
        _stage_lane_blocks(x_ref[...], xs_ref)
        gain = g_ref[...]
        for t in range(CHUNK):
            rows = _strided_rows(xs_ref, t, LANES, CHUNK)
            hn_ref[t * LANES:(t + 1) * LANES, :] = _rmsnorm_rows(rows, gain).astype(BF16)

    half = TILE // 2
    for part in range(2):
        cols = slice(part * half, (part + 1) * half)
        o_ref[:, cols] = lax.dot_general(wt_ref[...], hn_ref[cols, :], (((1,), (1,)), ((), ())),
                                         preferred_element_type=F32).astype(BF16)


def _ssm_inproj(x2d, gain, w_in_t):
    ntok = x2d.shape[0]
    return pl.pallas_call(
        _ssm_inproj_kernel,
        grid=(ntok // TILE, 2),
        in_specs=[
            pl.BlockSpec((TILE, D_MODEL), lambda i, j: (i, 0)),
            pl.BlockSpec((1, D_MODEL), lambda i, j: (0, 0)),
            pl.BlockSpec((D_MODEL, D_MODEL), lambda i, j: (j, 0)),
        ],
        out_specs=pl.BlockSpec((None, D_MODEL, TILE), lambda i, j: (j, 0, i)),
        out_shape=jax.ShapeDtypeStruct((2, D_MODEL, ntok), BF16),
        scratch_shapes=[pltpu.VMEM((TILE, D_MODEL), BF16), pltpu.VMEM((LANE_BLOCKS, TILE, LANES), F32)],
        compiler_params=_params("parallel", "arbitrary"),
        name="ssm_inproj",
    )(x2d, gain, w_in_t)


def _ssm_core_kernel(u_ref, mt_ref, w_ref, vt_ref, a16_ref, apl_ref, y_ref, ut_ref, s_ref, hf_ref, hb_ref,
                     *, n_seq, chunks_per_seq):
    n_tiles = n_seq * chunks_per_seq // LANES
    p = SSM_STATE
    sub = 8
    for tile in range(n_tiles):
        for t in range(CHUNK):
            src = (tile * CHUNK + t) * LANES
            ut_ref[t * SSM_GROUP:(t + 1) * SSM_GROUP, tile * LANES:(tile + 1) * LANES] = u_ref[:, src:src + LANES]
    ut = ut_ref[...]
    s_ref[...] = lax.dot_general(ut, w_ref[...], (((0,), (0,)), ((), ())), preferred_element_type=F32)

    lane = lax.broadcasted_iota(jnp.int32, (sub, LANES), 1)
    row = lax.broadcasted_iota(jnp.int32, (sub, LANES), 0)
    is_f = lane < p
    steps_in = jnp.where(is_f, row, sub - 1 - row)
    cur = (jnp.broadcast_to(a16_ref[0:1, :], (sub, LANES)), jnp.broadcast_to(a16_ref[1:2, :], (sub, LANES)))
    levels = []
    shift = 1
    while shift < sub:
        levels.append((shift, steps_in >= shift, cur))
        cur = _cmul(cur[0], cur[1], cur[0], cur[1])
        shift *= 2
    apl_re = apl_ref[0]
    apl_im = apl_ref[1]
    inner = steps_in >= 1
    n_steps = chunks_per_seq // sub

    def toward_scan_start(x, by):
        return jnp.where(is_f, pltpu.roll(x, by, axis=0), pltpu.roll(x, sub - by, axis=0))

    def seq_body(b, carry):
        seq0 = b * chunks_per_seq

        def step(i, h):
            h_re, h_im = h
            rf = pl.multiple_of(seq0 + i * sub, sub)
            rb = pl.multiple_of(seq0 + (n_steps - 1 - i) * sub, sub)
            x_re = jnp.where(is_f, s_ref[pl.ds(rf, sub), 0:LANES], s_ref[pl.ds(rb, sub), 0:LANES])
            x_im = jnp.where(is_f, s_ref[pl.ds(rf, sub), LANES:2 * LANES], s_ref[pl.ds(rb, sub), LANES:2 * LANES])
            for by, keep, (lr, li) in levels:
                d_re, d_im = _cmul(lr, li, jnp.where(keep, toward_scan_start(x_re, by), 0.0),
                                   jnp.where(keep, toward_scan_start(x_im, by), 0.0))
                x_re = x_re + d_re
                x_im = x_im + d_im
            c_re, c_im = _cmul(apl_re, apl_im, h_re, h_im)
            t_re = x_re + c_re
            t_im = x_im + c_im
            h0_re = jnp.where(inner, toward_scan_start(t_re, 1), h_re)
            h0_im = jnp.where(inner, toward_scan_start(t_im, 1), h_im)
            hf_ref[pl.ds(rf, sub), 0:LANES] = h0_re
            hf_ref[pl.ds(rf, sub), LANES:2 * LANES] = h0_im
            hb_ref[pl.ds(rb, sub), 0:LANES] = h0_re
            hb_ref[pl.ds(rb, sub), LANES:2 * LANES] = h0_im
            n_re = jnp.where(is_f[0:1], t_re[sub - 1:sub], t_re[0:1])
            n_im = jnp.where(is_f[0:1], t_im[sub - 1:sub], t_im[0:1])
            return n_re, n_im

        zero = jnp.zeros((1, LANES), F32)
        lax.fori_loop(0, n_steps, step, (zero, zero), unroll=4)
        return carry

    lax.fori_loop(0, n_seq, seq_body, 0)
    is_f_wide = lax.rem(lax.broadcasted_iota(jnp.int32, (1, 2 * LANES), 1), LANES) < p
    h0 = jnp.where(is_f_wide, hf_ref[...], hb_ref[...]).astype(BF16)
    yt = (jnp.dot(mt_ref[...], ut, preferred_element_type=F32)
          + lax.dot_general(vt_ref[...], h0, (((1,), (1,)), ((), ())), preferred_element_type=F32)).astype(BF16)
    for tile in range(n_tiles):
        for t in range(CHUNK):
            dst = (tile * CHUNK + t) * LANES
            y_ref[:, dst:dst + LANES] = yt[t * SSM_GROUP:(t + 1) * SSM_GROUP, tile * LANES:(tile + 1) * LANES]


def _ssm_core(uz_t, mats, batch, seq_len):
    m_t, w_n, v_t, a16, apl = mats
    ntok = batch * seq_len
    n_chunks = ntok // CHUNK
    mat_spec = pl.BlockSpec((None, CHUNK_COLS, CHUNK_COLS), lambda g: (g, 0, 0))
    return pl.pallas_call(
        functools.partial(_ssm_core_kernel, n_seq=batch, chunks_per_seq=seq_len // CHUNK),
        grid=(SSM_GROUPS,),
        in_specs=[pl.BlockSpec((None, SSM_GROUP, ntok), lambda g: (0, g, 0)),
                  mat_spec, mat_spec, mat_spec,
                  pl.BlockSpec((None, 2, LANES), lambda g: (g, 0, 0)),
                  pl.BlockSpec((None, 2, 8, LANES), lambda g: (g, 0, 0, 0))],
        out_specs=pl.BlockSpec((SSM_GROUP, ntok), lambda g: (g, 0)),
        out_shape=jax.ShapeDtypeStruct((D_MODEL, ntok), BF16),
        scratch_shapes=[pltpu.VMEM((CHUNK_COLS, n_chunks), BF16)] + [pltpu.VMEM((n_chunks, 2 * LANES), F32)] * 3,
        compiler_params=_params("parallel"),
        name="ssm_chunk_scan",
    )(uz_t, m_t, w_n, v_t, a16, apl)


def _ssm_glu_kernel(y_ref, z_ref, wg_ref, bg_ref, o_ref):
    gl = _gelu_tanh(y_ref[...].astype(F32))
    t = jnp.dot(wg_ref[...], gl.astype(BF16), preferred_element_type=F32) + bg_ref[...]
    y = gl * _sigmoid(t)
    o_ref[...] = (y * _silu(z_ref[...].astype(F32))).astype(BF16)


def _ssm_glu(y_t, uz_t, w_glu_t, b_glu_col):
    ntok = y_t.shape[1]
    tl = 1024
    return pl.pallas_call(
        _ssm_glu_kernel,
        grid=(ntok // tl,),
        in_specs=[pl.BlockSpec((D_MODEL, tl), lambda i: (0, i)),
                  pl.BlockSpec((None, D_MODEL, tl), lambda i: (1, 0, i)),
                  pl.BlockSpec((D_MODEL, D_MODEL), lambda i: (0, 0)),
                  pl.BlockSpec((D_MODEL, 1), lambda i: (0, 0))],
        out_specs=pl.BlockSpec((D_MODEL, tl), lambda i: (0, i)),
        out_shape=jax.ShapeDtypeStruct((D_MODEL, ntok), BF16),
        compiler_params=_params("parallel"),
        name="ssm_glu",
    )(y_t, uz_t, w_glu_t, b_glu_col)


def _ssm_out_kernel(y_ref, x_ref, w_ref, fg_ref, out_ref, ys_ref, *, final):
    half = TILE // 2
    for part in range(2):
        res = lax.dot_general(y_ref[:, part * half:(part + 1) * half], w_ref[...], (((0,), (0,)), ((), ())),
                              preferred_element_type=F32)
        for tt in range(CHUNK // 2):
            rows = pl.ds(part * (CHUNK // 2) + tt, LANES, stride=CHUNK)
            for c in range(LANE_BLOCKS):
                ys_ref[c, rows, :] = res[tt * LANES:(tt + 1) * LANES, c * LANES:(c + 1) * LANES]
    out = x_ref[...] + jnp.concatenate([ys_ref[c] for c in range(LANE_BLOCKS)], axis=1)
    if final:
        out = _rmsnorm_rows(out, fg_ref[...])
    out_ref[...] = out


def _ssm_out(y3_t, x2d, w_out, final_gain, final):
    ntok = x2d.shape[0]
    return pl.pallas_call(
        functools.partial(_ssm_out_kernel, final=final),
        grid=(ntok // TILE,),
        in_specs=[pl.BlockSpec((D_MODEL, TILE), lambda i: (0, i)),
                  pl.BlockSpec((TILE, D_MODEL), lambda i: (i, 0), pipeline_mode=pl.Buffered(1)),
                  pl.BlockSpec((D_MODEL, D_MODEL), lambda i: (0, 0)),
                  pl.BlockSpec((1, D_MODEL), lambda i: (0, 0))],
        out_specs=pl.BlockSpec((TILE, D_MODEL), lambda i: (i, 0)),
        out_shape=jax.ShapeDtypeStruct((ntok, D_MODEL), F32),
        scratch_shapes=[pltpu.VMEM((LANE_BLOCKS, TILE, LANES), F32)],
        compiler_params=_params("parallel"),
        name="ssm_outproj",
    )(y3_t, x2d, w_out, final_gain)


def _pair_heads(w):
    rows = w.shape[0]
    return w.reshape(rows, N_HEAD_PAIRS, 2, 2, HEAD_DIM // 2).transpose(0, 1, 3, 2, 4).reshape(rows, -1)


def _prep_attn_w_in(w_in):
    width = N_HEADS * HEAD_DIM
    blocks = [None] * N_ATTN_SLABS
    for g in range(len(GROUP_PATTERNS)):
        base = 3 * g * width
        blocks[ATTN_SLABS["q"][g]] = _pair_heads(w_in[:, base:base + width]) * (HEAD_DIM ** -0.5)
        blocks[ATTN_SLABS["k"][g]] = _pair_heads(w_in[:, base + width:base + 2 * width])
        blocks[ATTN_SLABS["v"][g]] = w_in[:, base + 2 * width:base + 3 * width]
    blocks[ATTN_SLABS["z"]] = w_in[:, 3 * len(GROUP_PATTERNS) * width:]
    return jnp.concatenate(blocks, axis=1).astype(BF16)


def _rope_tables(seq_len):
    inv_freq = ROPE_THETA ** (-jnp.arange(0, HEAD_DIM, 2, dtype=F32) / HEAD_DIM)
    ang = jnp.arange(seq_len, dtype=F32)[:, None] * inv_freq[None, :]
    cos, sin = jnp.cos(ang), jnp.sin(ang)
    cos_t = jnp.concatenate([cos] * 4, axis=1)
    sin_t = jnp.concatenate([-sin, -sin, sin, sin], axis=1)

    def orders(tab, identity):
        out = []
        for _, dilation in GROUP_PATTERNS:
            rpc = TILE // dilation
            out.append(tab.reshape(seq_len // TILE, rpc, dilation, LANES).transpose(0, 2, 1, 3).reshape(seq_len, LANES))
        out.append(jnp.full_like(tab, identity))
        return jnp.stack(out)

    return orders(cos_t, 1.0), orders(sin_t, 0.0)


def _prep_ssm(a_re, a_im, log_step, b_re, b_im, c_re, c_im, d):
    hi = lax.Precision.HIGHEST
    g_n, p_n, c_n, t_n = SSM_GROUPS, SSM_STATE, SSM_GROUP, CHUNK
    dt = jnp.exp(log_step)[..., None]
    ldr = a_re * dt
    ldi = a_im * dt
    abar_re = jnp.exp(ldr) * jnp.cos(ldi)
    abar_im = jnp.exp(ldr) * jnp.sin(ldi)
    den = a_re * a_re + a_im * a_im
    nr = abar_re - 1.0
    ni = abar_im
    f_re = (nr * a_re + ni * a_im) / den
    f_im = (ni * a_re - nr * a_im) / den
    bb_re = f_re[..., None] * b_re - f_im[..., None] * b_im
    bb_im = f_re[..., None] * b_im + f_im[..., None] * b_re
    n = jnp.arange(t_n + 1, dtype=F32)[:, None, None, None]
    pw_re = jnp.exp(n * ldr) * jnp.cos(n * ldi)
    pw_im = jnp.exp(n * ldr) * jnp.sin(n * ldi)
    ca_re = c_re[None] * pw_re[:, :, :, None, :] - c_im[None] * pw_im[:, :, :, None, :]
    ca_im = c_re[None] * pw_im[:, :, :, None, :] + c_im[None] * pw_re[:, :, :, None, :]
    kern = jnp.einsum('ldgcp,dgpk->ldgck', jnp.concatenate([ca_re, -ca_im], axis=-1),
                      jnp.concatenate([bb_re, bb_im], axis=2), precision=hi)
    kf, kb = kern[:t_n, 0], kern[:t_n, 1]
    center = kf[0] + kb[0] + jnp.eye(c_n, dtype=F32)[None] * d.reshape(g_n, c_n)[:, :, None]
    taps = jnp.concatenate([kf[:0:-1], center[None], kb[1:]], axis=0)
    taps = taps.transpose(1, 2, 0, 3).reshape(g_n, c_n, (2 * t_n - 1) * c_n)
    m_t = jnp.stack([taps[:, :, (t_n - 1 - t) * c_n:(t_n - 1 - t) * c_n + CHUNK_COLS] for t in range(t_n)], axis=1)
    m_t = m_t.reshape(g_n, CHUNK_COLS, CHUNK_COLS)

    def state_in(pw_r, pw_i, direction):
        re = pw_r[..., None] * bb_re[direction][None] - pw_i[..., None] * bb_im[direction][None]
        im = pw_r[..., None] * bb_im[direction][None] + pw_i[..., None] * bb_re[direction][None]
        to_cols = lambda x: x.transpose(1, 0, 3, 2).reshape(g_n, CHUNK_COLS, p_n)
        return to_cols(re), to_cols(im)

    rev = t_n - 1 - jnp.arange(t_n)
    wf_re, wf_im = state_in(pw_re[rev, 0], pw_im[rev, 0], 0)
    wb_re, wb_im = state_in(pw_re[:t_n, 1], pw_im[:t_n, 1], 1)
    w_mat = jnp.concatenate([wf_re, wb_re, wf_im, wb_im], axis=2)

    def state_out(ca_r, ca_i):
        to_rows = lambda x: x.transpose(1, 3, 0, 2).reshape(g_n, p_n, CHUNK_COLS)
        return to_rows(ca_r), to_rows(-ca_i)

    vf_re, vf_im = state_out(ca_re[1:t_n + 1, 0], ca_im[1:t_n + 1, 0])
    back = t_n - jnp.arange(t_n)
    vb_re, vb_im = state_out(ca_re[back, 1], ca_im[back, 1])
    v_mat = jnp.concatenate([vf_re, vb_re, vf_im, vb_im], axis=1)

    lanes = lambda f, b: jnp.concatenate([f, b], axis=-1)
    a16 = jnp.stack([lanes(pw_re[t_n, 0], pw_re[t_n, 1]), lanes(pw_im[t_n, 0], pw_im[t_n, 1])], axis=1)
    j_row = jnp.arange(8, dtype=F32)[None, :, None]
    n_f = t_n * (j_row + 1.0)
    n_b = t_n * (8.0 - j_row)
    pow_rows = lambda nn, dr: (jnp.exp(nn * ldr[dr][:, None, :]) * jnp.cos(nn * ldi[dr][:, None, :]),
                               jnp.exp(nn * ldr[dr][:, None, :]) * jnp.sin(nn * ldi[dr][:, None, :]))
    plf_re, plf_im = pow_rows(n_f, 0)
    plb_re, plb_im = pow_rows(n_b, 1)
    apl = jnp.stack([lanes(plf_re, plb_re), lanes(plf_im, plb_im)], axis=1)
    tr = lambda x: x.transpose(0, 2, 1).astype(BF16)
    return m_t.astype(BF16), w_mat.astype(BF16), tr(v_mat), a16, apl


def _attention_layer(x2d, batch, seq_len, params, rope):
    gain, w_in_p, w_out = params
    qkvz = _attn_inproj(x2d, gain, w_in_p, rope[0], rope[1], seq_len)
    y = _fused_attention(qkvz, batch, seq_len)
    return _attn_out(y, x2d, w_out)


def _ssm_layer(x2d, batch, seq_len, params, final_gain, final):
    gain, w_in_t, mats, w_glu_t, b_glu_col, w_out = params
    uz_t = _ssm_inproj(x2d, gain, w_in_t)
    y_t = _ssm_core(uz_t, mats, batch, seq_len)
    y3_t = _ssm_glu(y_t, uz_t, w_glu_t, b_glu_col)
    return _ssm_out(y3_t, x2d, w_out, final_gain, final)


def _trunk(x, attn_params, ssm_params, final_gain):
    batch, seq_len, _ = x.shape
    assert seq_len % TILE == 0
    rope = _rope_tables(seq_len)
    x2d = x.reshape(batch * seq_len, D_MODEL)
    for i in range(DEPTH):
        j = i // 2
        if i % 2 == 0:
            x2d = _attention_layer(x2d, batch, seq_len, attn_params[j], rope)
        else:
            x2d = _ssm_layer(x2d, batch, seq_len, ssm_params[j], final_gain, final=(i == DEPTH - 1))
    return x2d.reshape(batch, seq_len, D_MODEL)


def _prepare(attn_norm, attn_w_in, attn_w_out, ssm_norm, ssm_w_in, ssm_a_re, ssm_a_im, ssm_log_step, ssm_b_re,
             ssm_b_im, ssm_c_re, ssm_c_im, ssm_d, ssm_w_glu, ssm_b_glu, ssm_w_out, final_norm):
    attn_params = []
    for j in range(attn_norm.shape[0]):
        attn_params.append((attn_norm[j][None, :], _prep_attn_w_in(attn_w_in[j]), attn_w_out[j].astype(BF16)))
    ssm_params = []
    for j in range(ssm_norm.shape[0]):
        mats = _prep_ssm(ssm_a_re[j], ssm_a_im[j], ssm_log_step[j], ssm_b_re[j], ssm_b_im[j],
                         ssm_c_re[j], ssm_c_im[j], ssm_d[j])
        ssm_params.append((ssm_norm[j][None, :], ssm_w_in[j].T.astype(BF16), mats,
                           ssm_w_glu[j].T.astype(BF16), ssm_b_glu[j][:, None], ssm_w_out[j].astype(BF16)))
    return attn_params, ssm_params, final_norm[None, :]


def kernel(x_prompt, x_sample, attn_norm, attn_w_in, attn_w_out, ssm_norm, ssm_w_in, ssm_a_re, ssm_a_im, ssm_log_step, ssm_b_re, ssm_b_im, ssm_c_re, ssm_c_im, ssm_d, ssm_w_glu, ssm_b_glu, ssm_w_out, final_norm):
    attn_params, ssm_params, final_gain = _prepare(
        attn_norm, attn_w_in, attn_w_out, ssm_norm, ssm_w_in, ssm_a_re, ssm_a_im, ssm_log_step, ssm_b_re,
        ssm_b_im, ssm_c_re, ssm_c_im, ssm_d, ssm_w_glu, ssm_b_glu, ssm_w_out, final_norm)
    y_prompt = _trunk(x_prompt, attn_params, ssm_params, final_gain)
    y_sample = _trunk(x_sample, attn_params, ssm_params, final_gain)
    return (y_prompt, y_sample)
```

```python
import functools
import math

import jax
import jax.numpy as jnp
from jax import lax
from jax.experimental import pallas as pl
from jax.experimental.pallas import tpu as pltpu

F32 = jnp.float32
BF16 = jnp.bfloat16

D_MODEL = 1024
DEPTH = 4
GROUP_PATTERNS = ((128, 1), (512, 4), (2048, 16))
N_HEADS = 16
HEAD_DIM = 64
N_HEAD_PAIRS = N_HEADS // 2
LANES = 128
LANE_BLOCKS = D_MODEL // LANES
ROPE_THETA = 10000.0
SSM_GROUP = 16
SSM_GROUPS = D_MODEL // SSM_GROUP
SSM_STATE = 64
CHUNK = 16
CHUNK_COLS = CHUNK * SSM_GROUP
TILE = CHUNK * LANES
TQ = 128
NORM_EPS = 1e-6
VMEM_LIMIT = 56 * 1024 * 1024

ATTN_SLABS = {"q": (0, 4, 7), "k": (1, 5, 8), "v": (2, 6, 9), "z": 3}
N_ATTN_SLABS = 10


def _sigmoid(x):
    return 1.0 / (1.0 + jnp.exp(-x))


def _silu(x):
    return x * _sigmoid(x)


def _gelu_tanh(x):
    c = math.sqrt(2.0 / math.pi)
    return x * (0.5 * (1.0 + jnp.tanh(c * (x + 0.044715 * (x * x * x)))))


def _rmsnorm_rows(xf, gain):
    return xf * lax.rsqrt(jnp.mean(xf * xf, axis=-1, keepdims=True) + NORM_EPS) * gain


def _aligned(x, multiple):
    return x if isinstance(x, int) else pl.multiple_of(x, multiple)


def _clip(x, lo, hi):
    return min(max(x, lo), hi) if isinstance(x, int) else jnp.clip(x, lo, hi)


def _cmul(ar, ai, br, bi):
    return ar * br - ai * bi, ar * bi + ai * br


def _params(*semantics):
    return pltpu.CompilerParams(dimension_semantics=semantics, vmem_limit_bytes=VMEM_LIMIT)


def _stage_lane_blocks(x, slab_ref):
    for c in range(LANE_BLOCKS):
        slab_ref[c] = x[:, c * LANES:(c + 1) * LANES]


def _strided_rows(slab_ref, start, count, stride):
    return jnp.concatenate(
        [slab_ref[c, pl.ds(start, count, stride=stride), :] for c in range(LANE_BLOCKS)], axis=1)


def _attn_inproj_kernel(x_ref, g_ref, w_ref, cos_ref, sin_ref, o_ref, hn_ref, xs_ref):
    j = pl.program_id(1)

    @pl.when(j == 0)
    def _():
        hn = _rmsnorm_rows(x_ref[...], g_ref[...])
        _stage_lane_blocks(hn, xs_ref)
        hn_ref[...] = hn.astype(BF16)

    for g in (1, 2):
        @pl.when(j == ATTN_SLABS["q"][g])
        def _(dilation=GROUP_PATTERNS[g][1]):
            rpc = TILE // dilation
            for r in range(dilation):
                hn_ref[r * rpc:(r + 1) * rpc, :] = _strided_rows(xs_ref, r, rpc, dilation).astype(BF16)

    half = TILE // 2
    for part in range(2):
        rows = slice(part * half, (part + 1) * half)
        acc = jnp.dot(hn_ref[rows, :], w_ref[...], preferred_element_type=F32)
        c = cos_ref[rows, :]
        s = sin_ref[rows, :]
        for hp in range(N_HEAD_PAIRS):
            t = acc[:, hp * LANES:(hp + 1) * LANES]
            o_ref[hp, rows, :] = (t * c + pltpu.roll(t, 64, axis=1) * s).astype(BF16)


def _attn_inproj(x2d, gain, w_p, cos_t, sin_t, seq_len):
    ntok = x2d.shape[0]
    assert seq_len % TILE == 0
    pos_blocks = seq_len // TILE

    def table_map(i, j):
        order = jnp.where(j >= ATTN_SLABS["q"][2], 2, jnp.where(j >= ATTN_SLABS["q"][1], 1, 0))
        is_plain = functools.reduce(jnp.logical_or, [j == b for b in ATTN_SLABS["v"] + (ATTN_SLABS["z"],)])
        return (jnp.where(is_plain, len(GROUP_PATTERNS), order), i % pos_blocks, 0)

    return pl.pallas_call(
        _attn_inproj_kernel,
        grid=(ntok // TILE, N_ATTN_SLABS),
        in_specs=[
            pl.BlockSpec((TILE, D_MODEL), lambda i, j: (i, 0)),
            pl.BlockSpec((1, D_MODEL), lambda i, j: (0, 0)),
            pl.BlockSpec((D_MODEL, D_MODEL), lambda i, j: (0, j)),
            pl.BlockSpec((None, TILE, LANES), table_map),
            pl.BlockSpec((None, TILE, LANES), table_map),
        ],
        out_specs=pl.BlockSpec((None, N_HEAD_PAIRS, TILE, LANES), lambda i, j: (j, 0, i, 0)),
        out_shape=jax.ShapeDtypeStruct((N_ATTN_SLABS, N_HEAD_PAIRS, ntok, LANES), BF16),
        scratch_shapes=[pltpu.VMEM((TILE, D_MODEL), BF16), pltpu.VMEM((LANE_BLOCKS, TILE, LANES), F32)],
        compiler_params=_params("parallel", "arbitrary"),
        name="attn_inproj",
    )(x2d, gain, w_p, cos_t, sin_t)


def _fused_attn_kernel(q0, k0, v0, z_ref, q1, k1, v1, q2, k2, v2, y_ref, acc_ref, m_ref, l_ref, bias_ref,
                       p_all, m_all, *, seq_len):
    lane = lax.broadcasted_iota(jnp.int32, (1, LANES), 1)
    head_of_lane = lax.rem(lane, 64) // 32
    first_head_out = lane < HEAD_DIM
    groups = ((q0, k0, v0), (q1, k1, v1), (q2, k2, v2))

    tk_max = bias_ref.shape[2]
    row_i = lax.broadcasted_iota(jnp.int32, (TQ, tk_max), 0)
    col_i = lax.broadcasted_iota(jnp.int32, (TQ, tk_max), 1)
    for o in range(bias_ref.shape[0]):
        bias_ref[o] = jnp.where(jnp.abs(row_i - col_i + o * 64) <= 64, 0.0, -jnp.inf).astype(F32)

    n_windows = seq_len // TILE

    def window(w, carry):
        base = _aligned(w * TILE, TILE)
        for g, (q_ref, k_ref, v_ref) in enumerate(groups):
            win, dilation = GROUP_PATTERNS[g]
            half = win // (2 * dilation)
            length = seq_len // dilation
            rpc = TILE // dilation
            nqb = rpc // TQ
            tk = min(4 * half, length)
            n_pieces = tk // half
            n_blocks = dilation * nqb

            def locate(idx, dilation=dilation, half=half, length=length, rpc=rpc, nqb=nqb, tk=tk):
                r = idx // nqb
                lw = (idx % nqb) * TQ
                l0 = w * rpc + lw
                ks = _clip(l0 - half, 0, length - tk)
                return r, lw, l0, ks

            def key_rows(ref, r, ks, rpc=rpc, half=half, n_pieces=n_pieces):
                parts = []
                for p in range(n_pieces):
                    l = ks + p * half
                    row = _aligned((l // rpc) * TILE + r * rpc + l % rpc, half)
                    parts.append(ref[pl.ds(row, half), :])
                return jnp.concatenate(parts, axis=0)

            def probabilities(idx, q_ref=q_ref, k_ref=k_ref, rpc=rpc, half=half, tk=tk):
                r, lw, l0, ks = locate(idx)
                q = q_ref[pl.ds(_aligned(base + r * rpc + lw, TQ), TQ), :]
                k = key_rows(k_ref, r, ks)
                bias = bias_ref[(l0 - ks) // half][:, 0:tk]
                maxima = []
                for h in range(2):
                    q_h = jnp.where(head_of_lane == h, q, jnp.zeros_like(q))
                    s = lax.dot_general(q_h, k, (((1,), (1,)), ((), ())), preferred_element_type=F32) + bias
                    m = jnp.max(s, axis=-1, keepdims=True)
                    p_all[idx, h * TQ:(h + 1) * TQ, 0:tk] = jnp.exp(s - m).astype(BF16)
                    maxima.append(m)
                m_all[idx] = jnp.where(first_head_out, maxima[0], maxima[1])

            def values(idx, g=g, v_ref=v_ref, dilation=dilation, tk=tk):
                r, lw, l0, ks = locate(idx)
                v = key_rows(v_ref, r, ks)
                v_aug = jnp.concatenate([v, jnp.ones_like(v)], axis=1)
                pv2 = jnp.dot(p_all[idx, :, 0:tk], v_aug, preferred_element_type=F32)
                pv = jnp.where(first_head_out, pv2[:TQ, :LANES], pv2[TQ:, :LANES])
                dd = jnp.where(first_head_out, pv2[:TQ, LANES:], pv2[TQ:, LANES:])
                mm = m_all[idx]
                if dilation == 1:
                    rows = pl.ds(_aligned(lw, TQ), TQ)
                else:
                    rows = pl.ds(lw * dilation + r, TQ, stride=dilation)
                if g == 0:
                    acc_ref[rows, :] = pv
                    m_ref[rows, :] = mm
                    l_ref[rows, :] = dd
                else:
                    m_old = m_ref[rows, :]
                    m_new = jnp.maximum(m_old, mm)
                    e_old = jnp.exp(m_old - m_new)
                    e_new = jnp.exp(mm - m_new)
                    acc_ref[rows, :] = acc_ref[rows, :] * e_old + pv * e_new
                    l_ref[rows, :] = l_ref[rows, :] * e_old + dd * e_new
                    if g + 1 < len(groups):
                        m_ref[rows, :] = m_new

            for phase in (probabilities, values):
                for idx in range(n_blocks):
                    phase(idx)
        gate = _silu(z_ref[pl.ds(base, TILE), :].astype(F32))
        y_ref[pl.ds(base, TILE), :] = (acc_ref[...] / l_ref[...] * gate).astype(BF16)
        return carry

    if n_windows == 1:
        window(0, 0)
    else:
        lax.fori_loop(0, n_windows, window, 0)


def _fused_attention(qkvz, batch, seq_len):
    view = qkvz.reshape(N_ATTN_SLABS, N_HEAD_PAIRS, batch, seq_len, LANES)
    slab_bytes = seq_len * LANES * 2
    scratch_bytes = (3 * TILE * LANES + 3 * TQ * 4 * 64 + TILE * LANES) * 4 + TILE * 2 * 4 * 64 * 2
    double_buffered = 2 * (N_ATTN_SLABS + 1) * slab_bytes + scratch_bytes
    mode = {} if double_buffered <= VMEM_LIMIT - 4 * 1024 * 1024 else {"pipeline_mode": pl.Buffered(1)}

    def slab_spec(slab):
        return pl.BlockSpec((None, None, None, seq_len, LANES), lambda b, hp: (slab, hp, b, 0, 0), **mode)

    order = [ATTN_SLABS["q"][0], ATTN_SLABS["k"][0], ATTN_SLABS["v"][0], ATTN_SLABS["z"],
             ATTN_SLABS["q"][1], ATTN_SLABS["k"][1], ATTN_SLABS["v"][1],
             ATTN_SLABS["q"][2], ATTN_SLABS["k"][2], ATTN_SLABS["v"][2]]
    tk_max = min(4 * 64, seq_len // GROUP_PATTERNS[0][1])
    y = pl.pallas_call(
        functools.partial(_fused_attn_kernel, seq_len=seq_len),
        grid=(batch, N_HEAD_PAIRS),
        in_specs=[slab_spec(s) for s in order],
        out_specs=pl.BlockSpec((None, None, seq_len, LANES), lambda b, hp: (hp, b, 0, 0)),
        out_shape=jax.ShapeDtypeStruct((N_HEAD_PAIRS, batch, seq_len, LANES), BF16),
        scratch_shapes=[pltpu.VMEM((TILE, LANES), F32)] * 3 + [
            pltpu.VMEM((3, TQ, tk_max), F32),
            pltpu.VMEM((TILE // TQ, 2 * TQ, tk_max), BF16),
            pltpu.VMEM((TILE // TQ, TQ, LANES), F32)],
        compiler_params=_params("parallel", "parallel"),
        name="fused_banded_attn",
    )(*([view] * N_ATTN_SLABS))
    return y.reshape(N_HEAD_PAIRS, batch * seq_len, LANES)


def _attn_out_kernel(y_ref, x_ref, w_ref, out_ref):
    y = jnp.concatenate([y_ref[hp] for hp in range(N_HEAD_PAIRS)], axis=1)
    out_ref[...] = x_ref[...] + jnp.dot(y, w_ref[...], preferred_element_type=F32)


def _attn_out(y, x2d, w_out):
    ntok = x2d.shape[0]
    tm = 512
    return pl.pallas_call(
        _attn_out_kernel,
        grid=(ntok // tm,),
        in_specs=[
            pl.BlockSpec((N_HEAD_PAIRS, tm, LANES), lambda i: (0, i, 0)),
            pl.BlockSpec((tm, D_MODEL), lambda i: (i, 0)),
            pl.BlockSpec((D_MODEL, D_MODEL), lambda i: (0, 0)),
        ],
        out_specs=pl.BlockSpec((tm, D_MODEL), lambda i: (i, 0)),
        out_shape=jax.ShapeDtypeStruct((ntok, D_MODEL), F32),
        compiler_params=_params("parallel"),
        name="attn_outproj",
    )(y, x2d, w_out)


def _ssm_inproj_kernel(x_ref, g_ref, wt_ref, o_ref, hn_ref, xs_ref):
    _stage_lane_blocks(x_ref[...], xs_ref)
    gain = g_ref[...]
    half = TILE // 2
    for part in range(2):
        for t in range(part * CHUNK // 2, (part + 1) * CHUNK // 2):
            rows = _strided_rows(xs_ref, t, LANES, CHUNK)
            hn_ref[t * LANES:(t + 1) * LANES, :] = _rmsnorm_rows(rows, gain).astype(BF16)
        cols = slice(part * half, (part + 1) * half)
        for j in range(2):
            o_ref[j, :, cols] = lax.dot_general(wt_ref[j * D_MODEL:(j + 1) * D_MODEL, :], hn_ref[cols, :],
                                                (((1,), (1,)), ((), ())), preferred_element_type=F32).astype(BF16)


def _ssm_inproj(x2d, gain, w_in_t):
    ntok = x2d.shape[0]
    return pl.pallas_call(
        _ssm_inproj_kernel,
        grid=(ntok // TILE,),
        in_specs=[
            pl.BlockSpec((TILE, D_MODEL), lambda i: (i, 0)),
            pl.BlockSpec((1, D_MODEL), lambda i: (0, 0)),
            pl.BlockSpec((2 * D_MODEL, D_MODEL), lambda i: (0, 0), pipeline_mode=pl.Buffered(1)),
        ],
        out_specs=pl.BlockSpec((2, D_MODEL, TILE), lambda i: (0, 0, i)),
        out_shape=jax.ShapeDtypeStruct((2, D_MODEL, ntok), BF16),
        scratch_shapes=[pltpu.VMEM((TILE, D_MODEL), BF16), pltpu.VMEM((LANE_BLOCKS, TILE, LANES), F32)],
        compiler_params=_params("parallel"),
        name="ssm_inproj",
    )(x2d, gain, w_in_t)


def _ssm_core_kernel(u_ref, mt_ref, w_ref, vt_ref, a16_ref, apl_ref, y_ref, ut_ref, s_ref, hf_ref, hb_ref,
                     *, n_seq, chunks_per_seq):
    n_tiles = n_seq * chunks_per_seq // LANES
    p = SSM_STATE
    sub = 8
    for tile in range(n_tiles):
        for t in range(CHUNK):
            src = (tile * CHUNK + t) * LANES
            ut_ref[t * SSM_GROUP:(t + 1) * SSM_GROUP, tile * LANES:(tile + 1) * LANES] = u_ref[:, src:src + LANES]
    ut = ut_ref[...]
    s_ref[...] = lax.dot_general(ut, w_ref[...], (((0,), (0,)), ((), ())), preferred_element_type=F32)

    lane = lax.broadcasted_iota(jnp.int32, (sub, LANES), 1)
    row = lax.broadcasted_iota(jnp.int32, (sub, LANES), 0)
    is_f = lane < p
    steps_in = jnp.where(is_f, row, sub - 1 - row)
    cur = (jnp.broadcast_to(a16_ref[0:1, :], (sub, LANES)), jnp.broadcast_to(a16_ref[1:2, :], (sub, LANES)))
    levels = []
    shift = 1
    while shift < sub:
        levels.append((shift, steps_in >= shift, cur))
        cur = _cmul(cur[0], cur[1], cur[0], cur[1])
        shift *= 2
    apl_re = apl_ref[0]
    apl_im = apl_ref[1]
    inner = steps_in >= 1
    n_steps = chunks_per_seq // sub

    def toward_scan_start(x, by):
        return jnp.where(is_f, pltpu.roll(x, by, axis=0), pltpu.roll(x, sub - by, axis=0))

    def seq_body(b, carry):
        seq0 = b * chunks_per_seq

        def step(i, h):
            h_re, h_im = h
            rf = pl.multiple_of(seq0 + i * sub, sub)
            rb = pl.multiple_of(seq0 + (n_steps - 1 - i) * sub, sub)
            x_re = jnp.where(is_f, s_ref[pl.ds(rf, sub), 0:LANES], s_ref[pl.ds(rb, sub), 0:LANES])
            x_im = jnp.where(is_f, s_ref[pl.ds(rf, sub), LANES:2 * LANES], s_ref[pl.ds(rb, sub), LANES:2 * LANES])
            for by, keep, (lr, li) in levels:
                d_re, d_im = _cmul(lr, li, jnp.where(keep, toward_scan_start(x_re, by), 0.0),
                                   jnp.where(keep, toward_scan_start(x_im, by), 0.0))
                x_re = x_re + d_re
                x_im = x_im + d_im
            c_re, c_im = _cmul(apl_re, apl_im, h_re, h_im)
            t_re = x_re + c_re
            t_im = x_im + c_im
            h0_re = jnp.where(inner, toward_scan_start(t_re, 1), h_re)
            h0_im = jnp.where(inner, toward_scan_start(t_im, 1), h_im)
            hf_ref[pl.ds(rf, sub), 0:LANES] = h0_re
            hf_ref[pl.ds(rf, sub), LANES:2 * LANES] = h0_im
            hb_ref[pl.ds(rb, sub), 0:LANES] = h0_re
            hb_ref[pl.ds(rb, sub), LANES:2 * LANES] = h0_im
            n_re = jnp.where(is_f[0:1], t_re[sub - 1:sub], t_re[0:1])
            n_im = jnp.where(is_f[0:1], t_im[sub - 1:sub], t_im[0:1])
            return n_re, n_im

        zero = jnp.zeros((1, LANES), F32)
        lax.fori_loop(0, n_steps, step, (zero, zero), unroll=8)
        return carry

    lax.fori_loop(0, n_seq, seq_body, 0)
    is_f_wide = lax.rem(lax.broadcasted_iota(jnp.int32, (1, 2 * LANES), 1), LANES) < p
    h0 = jnp.where(is_f_wide, hf_ref[...], hb_ref[...]).astype(BF16)
    yt = (jnp.dot(mt_ref[...], ut, preferred_element_type=F32)
          + lax.dot_general(vt_ref[...], h0, (((1,), (1,)), ((), ())), preferred_element_type=F32)).astype(BF16)
    for tile in range(n_tiles):
        for t in range(CHUNK):
            dst = (tile * CHUNK + t) * LANES
            y_ref[:, dst:dst + LANES] = yt[t * SSM_GROUP:(t + 1) * SSM_GROUP, tile * LANES:(tile + 1) * LANES]


def _ssm_core(uz_t, mats, batch, seq_len):
    m_t, w_n, v_t, a16, apl = mats
    ntok = batch * seq_len
    n_chunks = ntok // CHUNK
    mat_spec = pl.BlockSpec((None, CHUNK_COLS, CHUNK_COLS), lambda g: (g, 0, 0))
    return pl.pallas_call(
        functools.partial(_ssm_core_kernel, n_seq=batch, chunks_per_seq=seq_len // CHUNK),
        grid=(SSM_GROUPS,),
        in_specs=[pl.BlockSpec((None, SSM_GROUP, ntok), lambda g: (0, g, 0)),
                  mat_spec, mat_spec, mat_spec,
                  pl.BlockSpec((None, 2, LANES), lambda g: (g, 0, 0)),
                  pl.BlockSpec((None, 2, 8, LANES), lambda g: (g, 0, 0, 0))],
        out_specs=pl.BlockSpec((SSM_GROUP, ntok), lambda g: (g, 0)),
        out_shape=jax.ShapeDtypeStruct((D_MODEL, ntok), BF16),
        scratch_shapes=[pltpu.VMEM((CHUNK_COLS, n_chunks), BF16)] + [pltpu.VMEM((n_chunks, 2 * LANES), F32)] * 3,
        compiler_params=_params("parallel"),
        name="ssm_chunk_scan",
    )(uz_t, m_t, w_n, v_t, a16, apl)


def _ssm_glu_kernel(y_ref, z_ref, wg_ref, bg_ref, o_ref):
    gl = _gelu_tanh(y_ref[...].astype(F32))
    t = jnp.dot(wg_ref[...], gl.astype(BF16), preferred_element_type=F32) + bg_ref[...]
    y = gl * _sigmoid(t)
    o_ref[...] = (y * _silu(z_ref[...].astype(F32))).astype(BF16)


def _ssm_glu(y_t, uz_t, w_glu_t, b_glu_col):
    ntok = y_t.shape[1]
    tl = 1024
    return pl.pallas_call(
        _ssm_glu_kernel,
        grid=(ntok // tl,),
        in_specs=[pl.BlockSpec((D_MODEL, tl), lambda i: (0, i)),
                  pl.BlockSpec((None, D_MODEL, tl), lambda i: (1, 0, i)),
                  pl.BlockSpec((D_MODEL, D_MODEL), lambda i: (0, 0)),
                  pl.BlockSpec((D_MODEL, 1), lambda i: (0, 0))],
        out_specs=pl.BlockSpec((D_MODEL, tl), lambda i: (0, i)),
        out_shape=jax.ShapeDtypeStruct((D_MODEL, ntok), BF16),
        compiler_params=_params("parallel"),
        name="ssm_glu",
    )(y_t, uz_t, w_glu_t, b_glu_col)


def _ssm_out_kernel(y_ref, x_ref, w_ref, fg_ref, out_ref, ys_ref, *, final):
    half = TILE // 2
    for part in range(2):
        res = lax.dot_general(y_ref[:, part * half:(part + 1) * half], w_ref[...], (((0,), (0,)), ((), ())),
                              preferred_element_type=F32)
        for tt in range(CHUNK // 2):
            rows = pl.ds(part * (CHUNK // 2) + tt, LANES, stride=CHUNK)
            for c in range(LANE_BLOCKS):
                ys_ref[c, rows, :] = res[tt * LANES:(tt + 1) * LANES, c * LANES:(c + 1) * LANES]
    out = x_ref[...] + jnp.concatenate([ys_ref[c] for c in range(LANE_BLOCKS)], axis=1)
    if final:
        out = _rmsnorm_rows(out, fg_ref[...])
    out_ref[...] = out


def _ssm_out(y3_t, x2d, w_out, final_gain, final):
    ntok = x2d.shape[0]
    return pl.pallas_call(
        functools.partial(_ssm_out_kernel, final=final),
        grid=(ntok // TILE,),
        in_specs=[pl.BlockSpec((D_MODEL, TILE), lambda i: (0, i)),
                  pl.BlockSpec((TILE, D_MODEL), lambda i: (i, 0), pipeline_mode=pl.Buffered(1)),
                  pl.BlockSpec((D_MODEL, D_MODEL), lambda i: (0, 0)),
                  pl.BlockSpec((1, D_MODEL), lambda i: (0, 0))],
        out_specs=pl.BlockSpec((TILE, D_MODEL), lambda i: (i, 0)),
        out_shape=jax.ShapeDtypeStruct((ntok, D_MODEL), F32),
        scratch_shapes=[pltpu.VMEM((LANE_BLOCKS, TILE, LANES), F32)],
        compiler_params=_params("parallel"),
        name="ssm_outproj",
    )(y3_t, x2d, w_out, final_gain)


def _pair_heads(w):
    rows = w.shape[0]
    return w.reshape(rows, N_HEAD_PAIRS, 2, 2, HEAD_DIM // 2).transpose(0, 1, 3, 2, 4).reshape(rows, -1)


def _prep_attn_w_in(w_in):
    width = N_HEADS * HEAD_DIM
    blocks = [None] * N_ATTN_SLABS
    for g in range(len(GROUP_PATTERNS)):
        base = 3 * g * width
        blocks[ATTN_SLABS["q"][g]] = _pair_heads(w_in[:, base:base + width]) * (HEAD_DIM ** -0.5)
        blocks[ATTN_SLABS["k"][g]] = _pair_heads(w_in[:, base + width:base + 2 * width])
        blocks[ATTN_SLABS["v"][g]] = w_in[:, base + 2 * width:base + 3 * width]
    blocks[ATTN_SLABS["z"]] = w_in[:, 3 * len(GROUP_PATTERNS) * width:]
    return jnp.concatenate(blocks, axis=1).astype(BF16)


def _rope_tables(seq_len):
    inv_freq = ROPE_THETA ** (-jnp.arange(0, HEAD_DIM, 2, dtype=F32) / HEAD_DIM)
    ang = jnp.arange(seq_len, dtype=F32)[:, None] * inv_freq[None, :]
    cos, sin = jnp.cos(ang), jnp.sin(ang)
    cos_t = jnp.concatenate([cos] * 4, axis=1)
    sin_t = jnp.concatenate([-sin, -sin, sin, sin], axis=1)

    def orders(tab, identity):
        out = []
        for _, dilation in GROUP_PATTERNS:
            rpc = TILE // dilation
            out.append(tab.reshape(seq_len // TILE, rpc, dilation, LANES).transpose(0, 2, 1, 3).reshape(seq_len, LANES))
        out.append(jnp.full_like(tab, identity))
        return jnp.stack(out)

    return orders(cos_t, 1.0), orders(sin_t, 0.0)


def _prep_ssm(a_re, a_im, log_step, b_re, b_im, c_re, c_im, d):
    hi = lax.Precision.HIGHEST
    g_n, p_n, c_n, t_n = SSM_GROUPS, SSM_STATE, SSM_GROUP, CHUNK
    dt = jnp.exp(log_step)[..., None]
    ldr = a_re * dt
    ldi = a_im * dt
    abar_re = jnp.exp(ldr) * jnp.cos(ldi)
    abar_im = jnp.exp(ldr) * jnp.sin(ldi)
    den = a_re * a_re + a_im * a_im
    nr = abar_re - 1.0
    ni = abar_im
    f_re = (nr * a_re + ni * a_im) / den
    f_im = (ni * a_re - nr * a_im) / den
    bb_re = f_re[..., None] * b_re - f_im[..., None] * b_im
    bb_im = f_re[..., None] * b_im + f_im[..., None] * b_re
    n = jnp.arange(t_n + 1, dtype=F32)[:, None, None, None]
    pw_re = jnp.exp(n * ldr) * jnp.cos(n * ldi)
    pw_im = jnp.exp(n * ldr) * jnp.sin(n * ldi)
    ca_re = c_re[None] * pw_re[:, :, :, None, :] - c_im[None] * pw_im[:, :, :, None, :]
    ca_im = c_re[None] * pw_im[:, :, :, None, :] + c_im[None] * pw_re[:, :, :, None, :]
    kern = jnp.einsum('ldgcp,dgpk->ldgck', jnp.concatenate([ca_re, -ca_im], axis=-1),
                      jnp.concatenate([bb_re, bb_im], axis=2), precision=hi)
    kf, kb = kern[:t_n, 0], kern[:t_n, 1]
    center = kf[0] + kb[0] + jnp.eye(c_n, dtype=F32)[None] * d.reshape(g_n, c_n)[:, :, None]
    taps = jnp.concatenate([kf[:0:-1], center[None], kb[1:]], axis=0)
    taps = taps.transpose(1, 2, 0, 3).reshape(g_n, c_n, (2 * t_n - 1) * c_n)
    m_t = jnp.stack([taps[:, :, (t_n - 1 - t) * c_n:(t_n - 1 - t) * c_n + CHUNK_COLS] for t in range(t_n)], axis=1)
    m_t = m_t.reshape(g_n, CHUNK_COLS, CHUNK_COLS)

    def state_in(pw_r, pw_i, direction):
        re = pw_r[..., None] * bb_re[direction][None] - pw_i[..., None] * bb_im[direction][None]
        im = pw_r[..., None] * bb_im[direction][None] + pw_i[..., None] * bb_re[direction][None]
        to_cols = lambda x: x.transpose(1, 0, 3, 2).reshape(g_n, CHUNK_COLS, p_n)
        return to_cols(re), to_cols(im)

    rev = t_n - 1 - jnp.arange(t_n)
    wf_re, wf_im = state_in(pw_re[rev, 0], pw_im[rev, 0], 0)
    wb_re, wb_im = state_in(pw_re[:t_n, 1], pw_im[:t_n, 1], 1)
    w_mat = jnp.concatenate([wf_re, wb_re, wf_im, wb_im], axis=2)

    def state_out(ca_r, ca_i):
        to_rows = lambda x: x.transpose(1, 3, 0, 2).reshape(g_n, p_n, CHUNK_COLS)
        return to_rows(ca_r), to_rows(-ca_i)

    vf_re, vf_im = state_out(ca_re[1:t_n + 1, 0], ca_im[1:t_n + 1, 0])
    back = t_n - jnp.arange(t_n)
    vb_re, vb_im = state_out(ca_re[back, 1], ca_im[back, 1])
    v_mat = jnp.concatenate([vf_re, vb_re, vf_im, vb_im], axis=1)

    lanes = lambda f, b: jnp.concatenate([f, b], axis=-1)
    a16 = jnp.stack([lanes(pw_re[t_n, 0], pw_re[t_n, 1]), lanes(pw_im[t_n, 0], pw_im[t_n, 1])], axis=1)
    j_row = jnp.arange(8, dtype=F32)[None, :, None]
    n_f = t_n * (j_row + 1.0)
    n_b = t_n * (8.0 - j_row)
    pow_rows = lambda nn, dr: (jnp.exp(nn * ldr[dr][:, None, :]) * jnp.cos(nn * ldi[dr][:, None, :]),
                               jnp.exp(nn * ldr[dr][:, None, :]) * jnp.sin(nn * ldi[dr][:, None, :]))
    plf_re, plf_im = pow_rows(n_f, 0)
    plb_re, plb_im = pow_rows(n_b, 1)
    apl = jnp.stack([lanes(plf_re, plb_re), lanes(plf_im, plb_im)], axis=1)
    tr = lambda x: x.transpose(0, 2, 1).astype(BF16)
    return m_t.astype(BF16), w_mat.astype(BF16), tr(v_mat), a16, apl


def _attention_layer(x2d, batch, seq_len, params, rope):
    gain, w_in_p, w_out = params
    qkvz = _attn_inproj(x2d, gain, w_in_p, rope[0], rope[1], seq_len)
    y = _fused_attention(qkvz, batch, seq_len)
    return _attn_out(y, x2d, w_out)


def _ssm_layer(x2d, batch, seq_len, params, final_gain, final):
    gain, w_in_t, mats, w_glu_t, b_glu_col, w_out = params
    uz_t = _ssm_inproj(x2d, gain, w_in_t)
    y_t = _ssm_core(uz_t, mats, batch, seq_len)
    y3_t = _ssm_glu(y_t, uz_t, w_glu_t, b_glu_col)
    return _ssm_out(y3_t, x2d, w_out, final_gain, final)


def _trunk(x, attn_params, ssm_params, final_gain):
    batch, seq_len, _ = x.shape
    assert seq_len % TILE == 0
    rope = _rope_tables(seq_len)
    x2d = x.reshape(batch * seq_len, D_MODEL)
    for i in range(DEPTH):
        j = i // 2
        if i % 2 == 0:
            x2d = _attention_layer(x2d, batch, seq_len, attn_params[j], rope)
        else:
            x2d = _ssm_layer(x2d, batch, seq_len, ssm_params[j], final_gain, final=(i == DEPTH - 1))
    return x2d.reshape(batch, seq_len, D_MODEL)


def _prepare(attn_norm, attn_w_in, attn_w_out, ssm_norm, ssm_w_in, ssm_a_re, ssm_a_im, ssm_log_step, ssm_b_re,
             ssm_b_im, ssm_c_re, ssm_c_im, ssm_d, ssm_w_glu, ssm_b_glu, ssm_w_out, final_norm):
    attn_params = []
    for j in range(attn_norm.shape[0]):
        attn_params.append((attn_norm[j][None, :], _prep_attn_w_in(attn_w_in[j]), attn_w_out[j].astype(BF16)))
    ssm_params = []
    for j in range(ssm_norm.shape[0]):
        mats = _prep_ssm(ssm_a_re[j], ssm_a_im[j], ssm_log_step[j], ssm_b_re[j], ssm_b_im[j],
                         ssm_c_re[j], ssm_c_im[j], ssm_d[j])
        ssm_params.append((ssm_norm[j][None, :], ssm_w_in[j].T.astype(BF16), mats,
                           ssm_w_glu[j].T.astype(BF16), ssm_b_glu[j][:, None], ssm_w_out[j].astype(BF16)))
    return attn_params, ssm_params, final_norm[None, :]


def kernel(x_prompt, x_sample, attn_norm, attn_w_in, attn_w_out, ssm_norm, ssm_w_in, ssm_a_re, ssm_a_im, ssm_log_step, ssm_b_re, ssm_b_im, ssm_c_re, ssm_c_im, ssm_d, ssm_w_glu, ssm_b_glu, ssm_w_out, final_norm):
    attn_params, ssm_params, final_gain = _prepare(
        attn_norm, attn_w_in, attn_w_out, ssm_norm, ssm_w_in, ssm_a_re, ssm_a_im, ssm_log_step, ssm_b_re,
        ssm_b_im, ssm_c_re, ssm_c_im, ssm_d, ssm_w_glu, ssm_b_glu, ssm_w_out, final_norm)
    y_prompt = _trunk(x_prompt, attn_params, ssm_params, final_gain)
    y_sample = _trunk(x_sample, attn_params, ssm_params, final_gain)
    return (y_prompt, y_sample)
```

```python
import functools
import math

import jax
import jax.numpy as jnp
from jax import lax
from jax.experimental import pallas as pl
from jax.experimental.pallas import tpu as pltpu

F32 = jnp.float32
BF16 = jnp.bfloat16

D_MODEL = 1024
DEPTH = 4
GROUP_PATTERNS = ((128, 1), (512, 4), (2048, 16))
N_HEADS = 16
HEAD_DIM = 64
N_HEAD_PAIRS = N_HEADS // 2
LANES = 128
LANE_BLOCKS = D_MODEL // LANES
ROPE_THETA = 10000.0
SSM_GROUP = 16
SSM_GROUPS = D_MODEL // SSM_GROUP
SSM_STATE = 64
CHUNK = 16
CHUNK_COLS = CHUNK * SSM_GROUP
TILE = CHUNK * LANES
TQ = 128
NORM_EPS = 1e-6
VMEM_LIMIT = 56 * 1024 * 1024

ATTN_SLABS = {"q": (0, 4, 7), "k": (1, 5, 8), "v": (2, 6, 9), "z": 3}
N_ATTN_SLABS = 10


def _sigmoid(x):
    return 1.0 / (1.0 + jnp.exp(-x))


def _silu(x):
    return x * _sigmoid(x)


def _gelu_tanh(x):
    c = math.sqrt(2.0 / math.pi)
    return x * (0.5 * (1.0 + jnp.tanh(c * (x + 0.044715 * (x * x * x)))))


def _rmsnorm_rows(xf, gain):
    return xf * lax.rsqrt(jnp.mean(xf * xf, axis=-1, keepdims=True) + NORM_EPS) * gain


def _aligned(x, multiple):
    return x if isinstance(x, int) else pl.multiple_of(x, multiple)


def _clip(x, lo, hi):
    return min(max(x, lo), hi) if isinstance(x, int) else jnp.clip(x, lo, hi)


def _cmul(ar, ai, br, bi):
    return ar * br - ai * bi, ar * bi + ai * br


def _params(*semantics):
    return pltpu.CompilerParams(dimension_semantics=semantics, vmem_limit_bytes=VMEM_LIMIT)


def _stage_lane_blocks(x, slab_ref):
    for c in range(LANE_BLOCKS):
        slab_ref[c] = x[:, c * LANES:(c + 1) * LANES]


def _strided_rows(slab_ref, start, count, stride):
    return jnp.concatenate(
        [slab_ref[c, pl.ds(start, count, stride=stride), :] for c in range(LANE_BLOCKS)], axis=1)


def _attn_inproj_kernel(x_ref, g_ref, w_ref, cos_ref, sin_ref, o_ref, hn_ref, xs_ref):
    j = pl.program_id(1)

    @pl.when(j == 0)
    def _():
        hn = _rmsnorm_rows(x_ref[...], g_ref[...])
        _stage_lane_blocks(hn, xs_ref)
        hn_ref[...] = hn.astype(BF16)

    for g in (1, 2):
        @pl.when(j == ATTN_SLABS["q"][g])
        def _(dilation=GROUP_PATTERNS[g][1]):
            rpc = TILE // dilation
            for r in range(dilation):
                hn_ref[r * rpc:(r + 1) * rpc, :] = _strided_rows(xs_ref, r, rpc, dilation).astype(BF16)

    half = TILE // 2
    for part in range(2):
        rows = slice(part * half, (part + 1) * half)
        acc = jnp.dot(hn_ref[rows, :], w_ref[...], preferred_element_type=F32)
        c = cos_ref[rows, :]
        s = sin_ref[rows, :]
        for hp in range(N_HEAD_PAIRS):
            t = acc[:, hp * LANES:(hp + 1) * LANES]
            o_ref[hp, rows, :] = (t * c + pltpu.roll(t, 64, axis=1) * s).astype(BF16)


def _attn_inproj(x2d, gain, w_p, cos_t, sin_t, seq_len):
    ntok = x2d.shape[0]
    assert seq_len % TILE == 0
    pos_blocks = seq_len // TILE

    def table_map(i, j):
        order = jnp.where(j >= ATTN_SLABS["q"][2], 2, jnp.where(j >= ATTN_SLABS["q"][1], 1, 0))
        is_plain = functools.reduce(jnp.logical_or, [j == b for b in ATTN_SLABS["v"] + (ATTN_SLABS["z"],)])
        return (jnp.where(is_plain, len(GROUP_PATTERNS), order), i % pos_blocks, 0)

    return pl.pallas_call(
        _attn_inproj_kernel,
        grid=(ntok // TILE, N_ATTN_SLABS),
        in_specs=[
            pl.BlockSpec((TILE, D_MODEL), lambda i, j: (i, 0)),
            pl.BlockSpec((1, D_MODEL), lambda i, j: (0, 0)),
            pl.BlockSpec((D_MODEL, D_MODEL), lambda i, j: (0, j)),
            pl.BlockSpec((None, TILE, LANES), table_map),
            pl.BlockSpec((None, TILE, LANES), table_map),
        ],
        out_specs=pl.BlockSpec((None, N_HEAD_PAIRS, TILE, LANES), lambda i, j: (j, 0, i, 0)),
        out_shape=jax.ShapeDtypeStruct((N_ATTN_SLABS, N_HEAD_PAIRS, ntok, LANES), BF16),
        scratch_shapes=[pltpu.VMEM((TILE, D_MODEL), BF16), pltpu.VMEM((LANE_BLOCKS, TILE, LANES), F32)],
        compiler_params=_params("parallel", "arbitrary"),
        name="attn_inproj",
    )(x2d, gain, w_p, cos_t, sin_t)


def _fused_attn_kernel(q0, k0, v0, z_ref, q1, k1, v1, q2, k2, v2, y_ref, acc_ref, m_ref, l_ref, bias_ref,
                       p_all, m_all, *, seq_len):
    lane = lax.broadcasted_iota(jnp.int32, (1, LANES), 1)
    head_of_lane = lax.rem(lane, 64) // 32
    first_head_out = lane < HEAD_DIM
    groups = ((q0, k0, v0), (q1, k1, v1), (q2, k2, v2))

    tk_max = bias_ref.shape[2]
    row_i = lax.broadcasted_iota(jnp.int32, (TQ, tk_max), 0)
    col_i = lax.broadcasted_iota(jnp.int32, (TQ, tk_max), 1)
    for o in range(bias_ref.shape[0]):
        bias_ref[o] = jnp.where(jnp.abs(row_i - col_i + o * 64) <= 64, 0.0, -jnp.inf).astype(F32)

    n_windows = seq_len // TILE

    def window(w, carry):
        base = _aligned(w * TILE, TILE)
        for g, (q_ref, k_ref, v_ref) in enumerate(groups):
            win, dilation = GROUP_PATTERNS[g]
            half = win // (2 * dilation)
            length = seq_len // dilation
            rpc = TILE // dilation
            nqb = rpc // TQ
            tk = min(4 * half, length)
            n_pieces = tk // half
            n_blocks = dilation * nqb

            def locate(idx, dilation=dilation, half=half, length=length, rpc=rpc, nqb=nqb, tk=tk):
                r = idx // nqb
                lw = (idx % nqb) * TQ
                l0 = w * rpc + lw
                ks = _clip(l0 - half, 0, length - tk)
                return r, lw, l0, ks

            def key_rows(ref, r, ks, rpc=rpc, half=half, n_pieces=n_pieces):
                parts = []
                for p in range(n_pieces):
                    l = ks + p * half
                    row = _aligned((l // rpc) * TILE + r * rpc + l % rpc, half)
                    parts.append(ref[pl.ds(row, half), :])
                return jnp.concatenate(parts, axis=0)

            def probabilities(idx, q_ref=q_ref, k_ref=k_ref, rpc=rpc, half=half, tk=tk):
                r, lw, l0, ks = locate(idx)
                q = q_ref[pl.ds(_aligned(base + r * rpc + lw, TQ), TQ), :]
                k = key_rows(k_ref, r, ks)
                bias = bias_ref[(l0 - ks) // half][:, 0:tk]
                maxima = []
                for h in range(2):
                    q_h = jnp.where(head_of_lane == h, q, jnp.zeros_like(q))
                    s = lax.dot_general(q_h, k, (((1,), (1,)), ((), ())), preferred_element_type=F32) + bias
                    m = jnp.max(s, axis=-1, keepdims=True)
                    p_all[idx, h * TQ:(h + 1) * TQ, 0:tk] = jnp.exp(s - m).astype(BF16)
                    maxima.append(m)
                m_all[idx] = jnp.where(first_head_out, maxima[0], maxima[1])

            def values(idx, g=g, v_ref=v_ref, dilation=dilation, tk=tk):
                r, lw, l0, ks = locate(idx)
                v = key_rows(v_ref, r, ks)
                v_aug = jnp.concatenate([v, jnp.ones_like(v)], axis=1)
                pv2 = jnp.dot(p_all[idx, :, 0:tk], v_aug, preferred_element_type=F32)
                pv = jnp.where(first_head_out, pv2[:TQ, :LANES], pv2[TQ:, :LANES])
                dd = jnp.where(first_head_out, pv2[:TQ, LANES:], pv2[TQ:, LANES:])
                mm = m_all[idx]
                if dilation == 1:
                    rows = pl.ds(_aligned(lw, TQ), TQ)
                else:
                    rows = pl.ds(lw * dilation + r, TQ, stride=dilation)
                if g == 0:
                    acc_ref[rows, :] = pv
                    m_ref[rows, :] = mm
                    l_ref[rows, :] = dd
                else:
                    m_old = m_ref[rows, :]
                    m_new = jnp.maximum(m_old, mm)
                    e_old = jnp.exp(m_old - m_new)
                    e_new = jnp.exp(mm - m_new)
                    acc_ref[rows, :] = acc_ref[rows, :] * e_old + pv * e_new
                    l_ref[rows, :] = l_ref[rows, :] * e_old + dd * e_new
                    if g + 1 < len(groups):
                        m_ref[rows, :] = m_new

            for phase in (probabilities, values):
                for idx in range(n_blocks):
                    phase(idx)
        gate = _silu(z_ref[pl.ds(base, TILE), :].astype(F32))
        y_ref[pl.ds(base, TILE), :] = (acc_ref[...] / l_ref[...] * gate).astype(BF16)
        return carry

    if n_windows == 1:
        window(0, 0)
    else:
        lax.fori_loop(0, n_windows, window, 0)


def _fused_attention(qkvz, batch, seq_len):
    view = qkvz.reshape(N_ATTN_SLABS, N_HEAD_PAIRS, batch, seq_len, LANES)
    slab_bytes = seq_len * LANES * 2
    scratch_bytes = (3 * TILE * LANES + 3 * TQ * 4 * 64 + TILE * LANES) * 4 + TILE * 2 * 4 * 64 * 2
    double_buffered = 2 * (N_ATTN_SLABS + 1) * slab_bytes + scratch_bytes
    mode = {} if double_buffered <= VMEM_LIMIT - 4 * 1024 * 1024 else {"pipeline_mode": pl.Buffered(1)}

    def slab_spec(slab):
        return pl.BlockSpec((None, None, None, seq_len, LANES), lambda b, hp: (slab, hp, b, 0, 0), **mode)

    order = [ATTN_SLABS["q"][0], ATTN_SLABS["k"][0], ATTN_SLABS["v"][0], ATTN_SLABS["z"],
             ATTN_SLABS["q"][1], ATTN_SLABS["k"][1], ATTN_SLABS["v"][1],
             ATTN_SLABS["q"][2], ATTN_SLABS["k"][2], ATTN_SLABS["v"][2]]
    tk_max = min(4 * 64, seq_len // GROUP_PATTERNS[0][1])
    y = pl.pallas_call(
        functools.partial(_fused_attn_kernel, seq_len=seq_len),
        grid=(batch, N_HEAD_PAIRS),
        in_specs=[slab_spec(s) for s in order],
        out_specs=pl.BlockSpec((None, None, seq_len, LANES), lambda b, hp: (hp, b, 0, 0)),
        out_shape=jax.ShapeDtypeStruct((N_HEAD_PAIRS, batch, seq_len, LANES), BF16),
        scratch_shapes=[pltpu.VMEM((TILE, LANES), F32)] * 3 + [
            pltpu.VMEM((3, TQ, tk_max), F32),
            pltpu.VMEM((TILE // TQ, 2 * TQ, tk_max), BF16),
            pltpu.VMEM((TILE // TQ, TQ, LANES), F32)],
        compiler_params=_params("parallel", "parallel"),
        name="fused_banded_attn",
    )(*([view] * N_ATTN_SLABS))
    return y.reshape(N_HEAD_PAIRS, batch * seq_len, LANES)


def _attn_out_kernel(y_ref, x_ref, w_ref, out_ref):
    y = jnp.concatenate([y_ref[hp] for hp in range(N_HEAD_PAIRS)], axis=1)
    out_ref[...] = x_ref[...] + jnp.dot(y, w_ref[...], preferred_element_type=F32)


def _attn_out(y, x2d, w_out):
    ntok = x2d.shape[0]
    tm = 1024
    return pl.pallas_call(
        _attn_out_kernel,
        grid=(ntok // tm,),
        in_specs=[
            pl.BlockSpec((N_HEAD_PAIRS, tm, LANES), lambda i: (0, i, 0)),
            pl.BlockSpec((tm, D_MODEL), lambda i: (i, 0)),
            pl.BlockSpec((D_MODEL, D_MODEL), lambda i: (0, 0)),
        ],
        out_specs=pl.BlockSpec((tm, D_MODEL), lambda i: (i, 0)),
        out_shape=jax.ShapeDtypeStruct((ntok, D_MODEL), F32),
        compiler_params=_params("parallel"),
        name="attn_outproj",
    )(y, x2d, w_out)


def _ssm_inproj_kernel(x_ref, g_ref, wt_ref, o_ref, hn_ref, xs_ref):
    _stage_lane_blocks(x_ref[...], xs_ref)
    gain = g_ref[...]
    half = TILE // 2
    for part in range(2):
        for t in range(part * CHUNK // 2, (part + 1) * CHUNK // 2):
            rows = _strided_rows(xs_ref, t, LANES, CHUNK)
            hn_ref[t * LANES:(t + 1) * LANES, :] = _rmsnorm_rows(rows, gain).astype(BF16)
        cols = slice(part * half, (part + 1) * half)
        for j in range(2):
            o_ref[j, :, cols] = lax.dot_general(wt_ref[j * D_MODEL:(j + 1) * D_MODEL, :], hn_ref[cols, :],
                                                (((1,), (1,)), ((), ())), preferred_element_type=F32).astype(BF16)


def _ssm_inproj(x2d, gain, w_in_t):
    ntok = x2d.shape[0]
    return pl.pallas_call(
        _ssm_inproj_kernel,
        grid=(ntok // TILE,),
        in_specs=[
            pl.BlockSpec((TILE, D_MODEL), lambda i: (i, 0)),
            pl.BlockSpec((1, D_MODEL), lambda i: (0, 0)),
            pl.BlockSpec((2 * D_MODEL, D_MODEL), lambda i: (0, 0), pipeline_mode=pl.Buffered(1)),
        ],
        out_specs=pl.BlockSpec((2, D_MODEL, TILE), lambda i: (0, 0, i)),
        out_shape=jax.ShapeDtypeStruct((2, D_MODEL, ntok), BF16),
        scratch_shapes=[pltpu.VMEM((TILE, D_MODEL), BF16), pltpu.VMEM((LANE_BLOCKS, TILE, LANES), F32)],
        compiler_params=_params("parallel"),
        name="ssm_inproj",
    )(x2d, gain, w_in_t)


def _ssm_core_kernel(u_ref, mt_ref, w_ref, vt_ref, a16_ref, apl_ref, y_ref, ut_ref, s_ref, hf_ref, hb_ref,
                     *, n_seq, chunks_per_seq):
    n_tiles = n_seq * chunks_per_seq // LANES
    p = SSM_STATE
    sub = 8
    for tile in range(n_tiles):
        for t in range(CHUNK):
            src = (tile * CHUNK + t) * LANES
            ut_ref[t * SSM_GROUP:(t + 1) * SSM_GROUP, tile * LANES:(tile + 1) * LANES] = u_ref[:, src:src + LANES]
    ut = ut_ref[...]
    s_ref[...] = lax.dot_general(ut, w_ref[...], (((0,), (0,)), ((), ())), preferred_element_type=F32)

    lane = lax.broadcasted_iota(jnp.int32, (sub, LANES), 1)
    row = lax.broadcasted_iota(jnp.int32, (sub, LANES), 0)
    is_f = lane < p
    steps_in = jnp.where(is_f, row, sub - 1 - row)
    cur = (jnp.broadcast_to(a16_ref[0:1, :], (sub, LANES)), jnp.broadcast_to(a16_ref[1:2, :], (sub, LANES)))
    levels = []
    shift = 1
    while shift < sub:
        levels.append((shift, steps_in >= shift, cur))
        cur = _cmul(cur[0], cur[1], cur[0], cur[1])
        shift *= 2
    apl_re = apl_ref[0]
    apl_im = apl_ref[1]
    inner = steps_in >= 1
    n_steps = chunks_per_seq // sub

    def toward_scan_start(x, by):
        return jnp.where(is_f, pltpu.roll(x, by, axis=0), pltpu.roll(x, sub - by, axis=0))

    def seq_body(b, carry):
        seq0 = b * chunks_per_seq

        def step(i, h):
            h_re, h_im = h
            rf = pl.multiple_of(seq0 + i * sub, sub)
            rb = pl.multiple_of(seq0 + (n_steps - 1 - i) * sub, sub)
            x_re = jnp.where(is_f, s_ref[pl.ds(rf, sub), 0:LANES], s_ref[pl.ds(rb, sub), 0:LANES])
            x_im = jnp.where(is_f, s_ref[pl.ds(rf, sub), LANES:2 * LANES], s_ref[pl.ds(rb, sub), LANES:2 * LANES])
            for by, keep, (lr, li) in levels:
                d_re, d_im = _cmul(lr, li, jnp.where(keep, toward_scan_start(x_re, by), 0.0),
                                   jnp.where(keep, toward_scan_start(x_im, by), 0.0))
                x_re = x_re + d_re
                x_im = x_im + d_im
            c_re, c_im = _cmul(apl_re, apl_im, h_re, h_im)
            t_re = x_re + c_re
            t_im = x_im + c_im
            h0_re = jnp.where(inner, toward_scan_start(t_re, 1), h_re)
            h0_im = jnp.where(inner, toward_scan_start(t_im, 1), h_im)
            hf_ref[pl.ds(rf, sub), 0:LANES] = h0_re
            hf_ref[pl.ds(rf, sub), LANES:2 * LANES] = h0_im
            hb_ref[pl.ds(rb, sub), 0:LANES] = h0_re
            hb_ref[pl.ds(rb, sub), LANES:2 * LANES] = h0_im
            n_re = jnp.where(is_f[0:1], t_re[sub - 1:sub], t_re[0:1])
            n_im = jnp.where(is_f[0:1], t_im[sub - 1:sub], t_im[0:1])
            return n_re, n_im

        zero = jnp.zeros((1, LANES), F32)
        lax.fori_loop(0, n_steps, step, (zero, zero), unroll=8)
        return carry

    lax.fori_loop(0, n_seq, seq_body, 0)
    is_f_wide = lax.rem(lax.broadcasted_iota(jnp.int32, (1, 2 * LANES), 1), LANES) < p
    h0 = jnp.where(is_f_wide, hf_ref[...], hb_ref[...]).astype(BF16)
    yt = (jnp.dot(mt_ref[...], ut, preferred_element_type=F32)
          + lax.dot_general(vt_ref[...], h0, (((1,), (1,)), ((), ())), preferred_element_type=F32)).astype(BF16)
    for tile in range(n_tiles):
        for t in range(CHUNK):
            dst = (tile * CHUNK + t) * LANES
            y_ref[:, dst:dst + LANES] = yt[t * SSM_GROUP:(t + 1) * SSM_GROUP, tile * LANES:(tile + 1) * LANES]


def _ssm_core(uz_t, mats, batch, seq_len):
    m_t, w_n, v_t, a16, apl = mats
    ntok = batch * seq_len
    n_chunks = ntok // CHUNK
    mat_spec = pl.BlockSpec((None, CHUNK_COLS, CHUNK_COLS), lambda g: (g, 0, 0))
    return pl.pallas_call(
        functools.partial(_ssm_core_kernel, n_seq=batch, chunks_per_seq=seq_len // CHUNK),
        grid=(SSM_GROUPS,),
        in_specs=[pl.BlockSpec((None, SSM_GROUP, ntok), lambda g: (0, g, 0)),
                  mat_spec, mat_spec, mat_spec,
                  pl.BlockSpec((None, 2, LANES), lambda g: (g, 0, 0)),
                  pl.BlockSpec((None, 2, 8, LANES), lambda g: (g, 0, 0, 0))],
        out_specs=pl.BlockSpec((SSM_GROUP, ntok), lambda g: (g, 0)),
        out_shape=jax.ShapeDtypeStruct((D_MODEL, ntok), BF16),
        scratch_shapes=[pltpu.VMEM((CHUNK_COLS, n_chunks), BF16)] + [pltpu.VMEM((n_chunks, 2 * LANES), F32)] * 3,
        compiler_params=_params("parallel"),
        name="ssm_chunk_scan",
    )(uz_t, m_t, w_n, v_t, a16, apl)


def _ssm_glu_kernel(y_ref, z_ref, wg_ref, bg_ref, o_ref):
    gl = _gelu_tanh(y_ref[...].astype(F32))
    t = jnp.dot(wg_ref[...], gl.astype(BF16), preferred_element_type=F32) + bg_ref[...]
    y = gl * _sigmoid(t)
    o_ref[...] = (y * _silu(z_ref[...].astype(F32))).astype(BF16)


def _ssm_glu(y_t, uz_t, w_glu_t, b_glu_col):
    ntok = y_t.shape[1]
    tl = 1024
    return pl.pallas_call(
        _ssm_glu_kernel,
        grid=(ntok // tl,),
        in_specs=[pl.BlockSpec((D_MODEL, tl), lambda i: (0, i)),
                  pl.BlockSpec((None, D_MODEL, tl), lambda i: (1, 0, i)),
                  pl.BlockSpec((D_MODEL, D_MODEL), lambda i: (0, 0)),
                  pl.BlockSpec((D_MODEL, 1), lambda i: (0, 0))],
        out_specs=pl.BlockSpec((D_MODEL, tl), lambda i: (0, i)),
        out_shape=jax.ShapeDtypeStruct((D_MODEL, ntok), BF16),
        compiler_params=_params("parallel"),
        name="ssm_glu",
    )(y_t, uz_t, w_glu_t, b_glu_col)


def _ssm_out_kernel(y_ref, x_ref, w_ref, fg_ref, out_ref, ys_ref, *, final):
    half = TILE // 2
    for part in range(2):
        res = lax.dot_general(y_ref[:, part * half:(part + 1) * half], w_ref[...], (((0,), (0,)), ((), ())),
                              preferred_element_type=F32)
        for tt in range(CHUNK // 2):
            rows = pl.ds(part * (CHUNK // 2) + tt, LANES, stride=CHUNK)
            for c in range(LANE_BLOCKS):
                ys_ref[c, rows, :] = res[tt * LANES:(tt + 1) * LANES, c * LANES:(c + 1) * LANES]
    out = x_ref[...] + jnp.concatenate([ys_ref[c] for c in range(LANE_BLOCKS)], axis=1)
    if final:
        out = _rmsnorm_rows(out, fg_ref[...])
    out_ref[...] = out


def _ssm_out(y3_t, x2d, w_out, final_gain, final):
    ntok = x2d.shape[0]
    return pl.pallas_call(
        functools.partial(_ssm_out_kernel, final=final),
        grid=(ntok // TILE,),
        in_specs=[pl.BlockSpec((D_MODEL, TILE), lambda i: (0, i)),
                  pl.BlockSpec((TILE, D_MODEL), lambda i: (i, 0)),
                  pl.BlockSpec((D_MODEL, D_MODEL), lambda i: (0, 0), pipeline_mode=pl.Buffered(1)),
                  pl.BlockSpec((1, D_MODEL), lambda i: (0, 0))],
        out_specs=pl.BlockSpec((TILE, D_MODEL), lambda i: (i, 0)),
        out_shape=jax.ShapeDtypeStruct((ntok, D_MODEL), F32),
        scratch_shapes=[pltpu.VMEM((LANE_BLOCKS, TILE, LANES), F32)],
        compiler_params=_params("parallel"),
        name="ssm_outproj",
    )(y3_t, x2d, w_out, final_gain)


def _pair_heads(w):
    rows = w.shape[0]
    return w.reshape(rows, N_HEAD_PAIRS, 2, 2, HEAD_DIM // 2).transpose(0, 1, 3, 2, 4).reshape(rows, -1)


def _prep_attn_w_in(w_in):
    width = N_HEADS * HEAD_DIM
    blocks = [None] * N_ATTN_SLABS
    for g in range(len(GROUP_PATTERNS)):
        base = 3 * g * width
        blocks[ATTN_SLABS["q"][g]] = _pair_heads(w_in[:, base:base + width]) * (HEAD_DIM ** -0.5)
        blocks[ATTN_SLABS["k"][g]] = _pair_heads(w_in[:, base + width:base + 2 * width])
        blocks[ATTN_SLABS["v"][g]] = w_in[:, base + 2 * width:base + 3 * width]
    blocks[ATTN_SLABS["z"]] = w_in[:, 3 * len(GROUP_PATTERNS) * width:]
    return jnp.concatenate(blocks, axis=1).astype(BF16)


def _rope_tables(seq_len):
    inv_freq = ROPE_THETA ** (-jnp.arange(0, HEAD_DIM, 2, dtype=F32) / HEAD_DIM)
    ang = jnp.arange(seq_len, dtype=F32)[:, None] * inv_freq[None, :]
    cos, sin = jnp.cos(ang), jnp.sin(ang)
    cos_t = jnp.concatenate([cos] * 4, axis=1)
    sin_t = jnp.concatenate([-sin, -sin, sin, sin], axis=1)

    def orders(tab, identity):
        out = []
        for _, dilation in GROUP_PATTERNS:
            rpc = TILE // dilation
            out.append(tab.reshape(seq_len // TILE, rpc, dilation, LANES).transpose(0, 2, 1, 3).reshape(seq_len, LANES))
        out.append(jnp.full_like(tab, identity))
        return jnp.stack(out)

    return orders(cos_t, 1.0), orders(sin_t, 0.0)


def _prep_ssm(a_re, a_im, log_step, b_re, b_im, c_re, c_im, d):
    hi = lax.Precision.HIGHEST
    g_n, p_n, c_n, t_n = SSM_GROUPS, SSM_STATE, SSM_GROUP, CHUNK
    dt = jnp.exp(log_step)[..., None]
    ldr = a_re * dt
    ldi = a_im * dt
    abar_re = jnp.exp(ldr) * jnp.cos(ldi)
    abar_im = jnp.exp(ldr) * jnp.sin(ldi)
    den = a_re * a_re + a_im * a_im
    nr = abar_re - 1.0
    ni = abar_im
    f_re = (nr * a_re + ni * a_im) / den
    f_im = (ni * a_re - nr * a_im) / den
    bb_re = f_re[..., None] * b_re - f_im[..., None] * b_im
    bb_im = f_re[..., None] * b_im + f_im[..., None] * b_re
    n = jnp.arange(t_n + 1, dtype=F32)[:, None, None, None]
    pw_re = jnp.exp(n * ldr) * jnp.cos(n * ldi)
    pw_im = jnp.exp(n * ldr) * jnp.sin(n * ldi)
    ca_re = c_re[None] * pw_re[:, :, :, None, :] - c_im[None] * pw_im[:, :, :, None, :]
    ca_im = c_re[None] * pw_im[:, :, :, None, :] + c_im[None] * pw_re[:, :, :, None, :]
    kern = jnp.einsum('ldgcp,dgpk->ldgck', jnp.concatenate([ca_re, -ca_im], axis=-1),
                      jnp.concatenate([bb_re, bb_im], axis=2), precision=hi)
    kf, kb = kern[:t_n, 0], kern[:t_n, 1]
    center = kf[0] + kb[0] + jnp.eye(c_n, dtype=F32)[None] * d.reshape(g_n, c_n)[:, :, None]
    taps = jnp.concatenate([kf[:0:-1], center[None], kb[1:]], axis=0)
    taps = taps.transpose(1, 2, 0, 3).reshape(g_n, c_n, (2 * t_n - 1) * c_n)
    m_t = jnp.stack([taps[:, :, (t_n - 1 - t) * c_n:(t_n - 1 - t) * c_n + CHUNK_COLS] for t in range(t_n)], axis=1)
    m_t = m_t.reshape(g_n, CHUNK_COLS, CHUNK_COLS)

    bbt_re = bb_re.transpose(0, 1, 3, 2)
    bbt_im = bb_im.transpose(0, 1, 3, 2)

    def state_in(pw_r, pw_i, direction):
        pr = pw_r.transpose(1, 0, 2)[:, :, None, :]
        pi = pw_i.transpose(1, 0, 2)[:, :, None, :]
        br, bi = bbt_re[direction][:, None], bbt_im[direction][:, None]
        return ((pr * br - pi * bi).reshape(g_n, CHUNK_COLS, p_n), (pr * bi + pi * br).reshape(g_n, CHUNK_COLS, p_n))

    wf_re, wf_im = state_in(pw_re[t_n - 1::-1, 0], pw_im[t_n - 1::-1, 0], 0)
    wb_re, wb_im = state_in(pw_re[:t_n, 1], pw_im[:t_n, 1], 1)
    w_mat = jnp.concatenate([wf_re, wb_re, wf_im, wb_im], axis=2)

    def state_out_t(ca_r, ca_i):
        rows = lambda x: x.transpose(1, 0, 2, 3).reshape(g_n, CHUNK_COLS, p_n)
        return rows(ca_r), rows(-ca_i)

    vf_re, vf_im = state_out_t(ca_re[1:t_n + 1, 0], ca_im[1:t_n + 1, 0])
    vb_re, vb_im = state_out_t(ca_re[t_n:0:-1, 1], ca_im[t_n:0:-1, 1])
    v_t = jnp.concatenate([vf_re, vb_re, vf_im, vb_im], axis=2)

    lanes = lambda f, b: jnp.concatenate([f, b], axis=-1)
    a16 = jnp.stack([lanes(pw_re[t_n, 0], pw_re[t_n, 1]), lanes(pw_im[t_n, 0], pw_im[t_n, 1])], axis=1)
    j_row = jnp.arange(8, dtype=F32)[None, :, None]
    n_f = t_n * (j_row + 1.0)
    n_b = t_n * (8.0 - j_row)
    pow_rows = lambda nn, dr: (jnp.exp(nn * ldr[dr][:, None, :]) * jnp.cos(nn * ldi[dr][:, None, :]),
                               jnp.exp(nn * ldr[dr][:, None, :]) * jnp.sin(nn * ldi[dr][:, None, :]))
    plf_re, plf_im = pow_rows(n_f, 0)
    plb_re, plb_im = pow_rows(n_b, 1)
    apl = jnp.stack([lanes(plf_re, plb_re), lanes(plf_im, plb_im)], axis=1)
    return m_t.astype(BF16), w_mat.astype(BF16), v_t.astype(BF16), a16, apl


def _attention_layer(x2d, batch, seq_len, params, rope):
    gain, w_in_p, w_out = params
    qkvz = _attn_inproj(x2d, gain, w_in_p, rope[0], rope[1], seq_len)
    y = _fused_attention(qkvz, batch, seq_len)
    return _attn_out(y, x2d, w_out)


def _ssm_layer(x2d, batch, seq_len, params, final_gain, final):
    gain, w_in_t, mats, w_glu_t, b_glu_col, w_out = params
    uz_t = _ssm_inproj(x2d, gain, w_in_t)
    y_t = _ssm_core(uz_t, mats, batch, seq_len)
    y3_t = _ssm_glu(y_t, uz_t, w_glu_t, b_glu_col)
    return _ssm_out(y3_t, x2d, w_out, final_gain, final)


def _trunk(x, attn_params, ssm_params, final_gain):
    batch, seq_len, _ = x.shape
    assert seq_len % TILE == 0
    rope = _rope_tables(seq_len)
    x2d = x.reshape(batch * seq_len, D_MODEL)
    for i in range(DEPTH):
        j = i // 2
        if i % 2 == 0:
            x2d = _attention_layer(x2d, batch, seq_len, attn_params[j], rope)
        else:
            x2d = _ssm_layer(x2d, batch, seq_len, ssm_params[j], final_gain, final=(i == DEPTH - 1))
    return x2d.reshape(batch, seq_len, D_MODEL)


def _prepare(attn_norm, attn_w_in, attn_w_out, ssm_norm, ssm_w_in, ssm_a_re, ssm_a_im, ssm_log_step, ssm_b_re,
             ssm_b_im, ssm_c_re, ssm_c_im, ssm_d, ssm_w_glu, ssm_b_glu, ssm_w_out, final_norm):
    attn_params = []
    for j in range(attn_norm.shape[0]):
        attn_params.append((attn_norm[j][None, :], _prep_attn_w_in(attn_w_in[j]), attn_w_out[j].astype(BF16)))
    ssm_params = []
    for j in range(ssm_norm.shape[0]):
        mats = _prep_ssm(ssm_a_re[j], ssm_a_im[j], ssm_log_step[j], ssm_b_re[j], ssm_b_im[j],
                         ssm_c_re[j], ssm_c_im[j], ssm_d[j])
        ssm_params.append((ssm_norm[j][None, :], ssm_w_in[j].T.astype(BF16), mats,
                           ssm_w_glu[j].T.astype(BF16), ssm_b_glu[j][:, None], ssm_w_out[j].astype(BF16)))
    return attn_params, ssm_params, final_norm[None, :]


def kernel(x_prompt, x_sample, attn_norm, attn_w_in, attn_w_out, ssm_norm, ssm_w_in, ssm_a_re, ssm_a_im, ssm_log_step, ssm_b_re, ssm_b_im, ssm_c_re, ssm_c_im, ssm_d, ssm_w_glu, ssm_b_glu, ssm_w_out, final_norm):
    attn_params, ssm_params, final_gain = _prepare(
        attn_norm, attn_w_in, attn_w_out, ssm_norm, ssm_w_in, ssm_a_re, ssm_a_im, ssm_log_step, ssm_b_re,
        ssm_b_im, ssm_c_re, ssm_c_im, ssm_d, ssm_w_glu, ssm_b_glu, ssm_w_out, final_norm)
    y_prompt = _trunk(x_prompt, attn_params, ssm_params, final_gain)
    y_sample = _trunk(x_sample, attn_params, ssm_params, final_gain)
    return (y_prompt, y_sample)
```

```python
import functools
import math

import jax
import jax.numpy as jnp
from jax import lax
from jax.experimental import pallas as pl
from jax.experimental.pallas import tpu as pltpu

F32 = jnp.float32
BF16 = jnp.bfloat16

D_MODEL = 1024
DEPTH = 4
GROUP_PATTERNS = ((128, 1), (512, 4), (2048, 16))
N_HEADS = 16
HEAD_DIM = 64
N_HEAD_PAIRS = N_HEADS // 2
LANES = 128
LANE_BLOCKS = D_MODEL // LANES
ROPE_THETA = 10000.0
SSM_GROUP = 16
SSM_GROUPS = D_MODEL // SSM_GROUP
SSM_STATE = 64
CHUNK = 16
CHUNK_COLS = CHUNK * SSM_GROUP
TILE = CHUNK * LANES
TQ = 128
NORM_EPS = 1e-6
VMEM_LIMIT = 56 * 1024 * 1024

ATTN_SLABS = {"q": (0, 4, 7), "k": (1, 5, 8), "v": (2, 6, 9), "z": 3}
N_ATTN_SLABS = 10


def _sigmoid(x):
    return 1.0 / (1.0 + jnp.exp(-x))


def _silu(x):
    return x * _sigmoid(x)


def _gelu_tanh(x):
    c = math.sqrt(2.0 / math.pi)
    return x * (0.5 * (1.0 + jnp.tanh(c * (x + 0.044715 * (x * x * x)))))


def _rmsnorm_rows(xf, gain):
    return xf * lax.rsqrt(jnp.mean(xf * xf, axis=-1, keepdims=True) + NORM_EPS) * gain


def _aligned(x, multiple):
    return x if isinstance(x, int) else pl.multiple_of(x, multiple)


def _clip(x, lo, hi):
    return min(max(x, lo), hi) if isinstance(x, int) else jnp.clip(x, lo, hi)


def _cmul(ar, ai, br, bi):
    return ar * br - ai * bi, ar * bi + ai * br


def _params(*semantics):
    return pltpu.CompilerParams(dimension_semantics=semantics, vmem_limit_bytes=VMEM_LIMIT)


def _stage_lane_blocks(x, slab_ref):
    for c in range(LANE_BLOCKS):
        slab_ref[c] = x[:, c * LANES:(c + 1) * LANES]


def _strided_rows(slab_ref, start, count, stride):
    return jnp.concatenate(
        [slab_ref[c, pl.ds(start, count, stride=stride), :] for c in range(LANE_BLOCKS)], axis=1)


def _attn_inproj_kernel(x_ref, g_ref, w_ref, cos_ref, sin_ref, o_ref, hn_ref, xs_ref):
    j = pl.program_id(1)

    @pl.when(j == 0)
    def _():
        hn = _rmsnorm_rows(x_ref[...], g_ref[...])
        _stage_lane_blocks(hn, xs_ref)
        hn_ref[...] = hn.astype(BF16)

    for g in (1, 2):
        @pl.when(j == ATTN_SLABS["q"][g])
        def _(dilation=GROUP_PATTERNS[g][1]):
            rpc = TILE // dilation
            for r in range(dilation):
                hn_ref[r * rpc:(r + 1) * rpc, :] = _strided_rows(xs_ref, r, rpc, dilation).astype(BF16)

    half = TILE // 2
    for part in range(2):
        rows = slice(part * half, (part + 1) * half)
        acc = jnp.dot(hn_ref[rows, :], w_ref[...], preferred_element_type=F32)
        c = cos_ref[rows, :]
        s = sin_ref[rows, :]
        for hp in range(N_HEAD_PAIRS):
            t = acc[:, hp * LANES:(hp + 1) * LANES]
            o_ref[hp, rows, :] = (t * c + pltpu.roll(t, 64, axis=1) * s).astype(BF16)


def _attn_inproj(x2d, gain, w_p, cos_t, sin_t, seq_len):
    ntok = x2d.shape[0]
    assert seq_len % TILE == 0
    pos_blocks = seq_len // TILE

    def table_map(i, j):
        order = jnp.where(j >= ATTN_SLABS["q"][2], 2, jnp.where(j >= ATTN_SLABS["q"][1], 1, 0))
        is_plain = functools.reduce(jnp.logical_or, [j == b for b in ATTN_SLABS["v"] + (ATTN_SLABS["z"],)])
        return (jnp.where(is_plain, len(GROUP_PATTERNS), order), i % pos_blocks, 0)

    return pl.pallas_call(
        _attn_inproj_kernel,
        grid=(ntok // TILE, N_ATTN_SLABS),
        in_specs=[
            pl.BlockSpec((TILE, D_MODEL), lambda i, j: (i, 0)),
            pl.BlockSpec((1, D_MODEL), lambda i, j: (0, 0)),
            pl.BlockSpec((D_MODEL, D_MODEL), lambda i, j: (0, j)),
            pl.BlockSpec((None, TILE, LANES), table_map),
            pl.BlockSpec((None, TILE, LANES), table_map),
        ],
        out_specs=pl.BlockSpec((None, N_HEAD_PAIRS, TILE, LANES), lambda i, j: (j, 0, i, 0)),
        out_shape=jax.ShapeDtypeStruct((N_ATTN_SLABS, N_HEAD_PAIRS, ntok, LANES), BF16),
        scratch_shapes=[pltpu.VMEM((TILE, D_MODEL), BF16), pltpu.VMEM((LANE_BLOCKS, TILE, LANES), F32)],
        compiler_params=_params("parallel", "arbitrary"),
        name="attn_inproj",
    )(x2d, gain, w_p, cos_t, sin_t)


def _fused_attn_kernel(q0, k0, v0, z_ref, q1, k1, v1, q2, k2, v2, y_ref, acc_ref, m_ref, l_ref, bias_ref,
                       p_all, m_all, *, seq_len):
    lane = lax.broadcasted_iota(jnp.int32, (1, LANES), 1)
    head_of_lane = lax.rem(lane, 64) // 32
    first_head_out = lane < HEAD_DIM
    groups = ((q0, k0, v0), (q1, k1, v1), (q2, k2, v2))

    tk_max = bias_ref.shape[2]
    row_i = lax.broadcasted_iota(jnp.int32, (TQ, tk_max), 0)
    col_i = lax.broadcasted_iota(jnp.int32, (TQ, tk_max), 1)
    for o in range(bias_ref.shape[0]):
        bias_ref[o] = jnp.where(jnp.abs(row_i - col_i + o * 64) <= 64, 0.0, -jnp.inf).astype(F32)

    n_windows = seq_len // TILE

    def window(w, carry):
        base = _aligned(w * TILE, TILE)
        for g, (q_ref, k_ref, v_ref) in enumerate(groups):
            win, dilation = GROUP_PATTERNS[g]
            half = win // (2 * dilation)
            length = seq_len // dilation
            rpc = TILE // dilation
            nqb = rpc // TQ
            tk = min(4 * half, length)
            n_pieces = tk // half
            n_blocks = dilation * nqb

            def locate(idx, dilation=dilation, half=half, length=length, rpc=rpc, nqb=nqb, tk=tk):
                r = idx // nqb
                lw = (idx % nqb) * TQ
                l0 = w * rpc + lw
                ks = _clip(l0 - half, 0, length - tk)
                return r, lw, l0, ks

            def key_rows(ref, r, ks, rpc=rpc, half=half, n_pieces=n_pieces):
                parts = []
                for p in range(n_pieces):
                    l = ks + p * half
                    row = _aligned((l // rpc) * TILE + r * rpc + l % rpc, half)
                    parts.append(ref[pl.ds(row, half), :])
                return jnp.concatenate(parts, axis=0)

            def probabilities(idx, q_ref=q_ref, k_ref=k_ref, rpc=rpc, half=half, tk=tk):
                r, lw, l0, ks = locate(idx)
                q = q_ref[pl.ds(_aligned(base + r * rpc + lw, TQ), TQ), :]
                k = key_rows(k_ref, r, ks)
                bias = bias_ref[(l0 - ks) // half][:, 0:tk]
                maxima = []
                for h in range(2):
                    q_h = jnp.where(head_of_lane == h, q, jnp.zeros_like(q))
                    s = lax.dot_general(q_h, k, (((1,), (1,)), ((), ())), preferred_element_type=F32) + bias
                    m = jnp.max(s, axis=-1, keepdims=True)
                    p_all[idx, h * TQ:(h + 1) * TQ, 0:tk] = jnp.exp(s - m).astype(BF16)
                    maxima.append(m)
                m_all[idx] = jnp.where(first_head_out, maxima[0], maxima[1])

            def values(idx, g=g, v_ref=v_ref, dilation=dilation, tk=tk):
                r, lw, l0, ks = locate(idx)
                v = key_rows(v_ref, r, ks)
                v_aug = jnp.concatenate([v, jnp.ones_like(v)], axis=1)
                pv2 = jnp.dot(p_all[idx, :, 0:tk], v_aug, preferred_element_type=F32)
                pv = jnp.where(first_head_out, pv2[:TQ, :LANES], pv2[TQ:, :LANES])
                dd = jnp.where(first_head_out, pv2[:TQ, LANES:], pv2[TQ:, LANES:])
                mm = m_all[idx]
                if dilation == 1:
                    rows = pl.ds(_aligned(lw, TQ), TQ)
                else:
                    rows = pl.ds(lw * dilation + r, TQ, stride=dilation)
                if g == 0:
                    acc_ref[rows, :] = pv
                    m_ref[rows, :] = mm
                    l_ref[rows, :] = dd
                else:
                    m_old = m_ref[rows, :]
                    m_new = jnp.maximum(m_old, mm)
                    e_old = jnp.exp(m_old - m_new)
                    e_new = jnp.exp(mm - m_new)
                    acc_ref[rows, :] = acc_ref[rows, :] * e_old + pv * e_new
                    l_ref[rows, :] = l_ref[rows, :] * e_old + dd * e_new
                    if g + 1 < len(groups):
                        m_ref[rows, :] = m_new

            for phase in (probabilities, values):
                for idx in range(n_blocks):
                    phase(idx)
        gate = _silu(z_ref[pl.ds(base, TILE), :].astype(F32))
        y_ref[pl.ds(base, TILE), :] = (acc_ref[...] / l_ref[...] * gate).astype(BF16)
        return carry

    if n_windows == 1:
        window(0, 0)
    else:
        lax.fori_loop(0, n_windows, window, 0)


def _fused_attention(qkvz, batch, seq_len):
    view = qkvz.reshape(N_ATTN_SLABS, N_HEAD_PAIRS, batch, seq_len, LANES)
    slab_bytes = seq_len * LANES * 2
    scratch_bytes = (3 * TILE * LANES + 3 * TQ * 4 * 64 + TILE * LANES) * 4 + TILE * 2 * 4 * 64 * 2
    double_buffered = 2 * (N_ATTN_SLABS + 1) * slab_bytes + scratch_bytes
    mode = {} if double_buffered <= VMEM_LIMIT - 4 * 1024 * 1024 else {"pipeline_mode": pl.Buffered(1)}

    def slab_spec(slab):
        return pl.BlockSpec((None, None, None, seq_len, LANES), lambda b, hp: (slab, hp, b, 0, 0), **mode)

    order = [ATTN_SLABS["q"][0], ATTN_SLABS["k"][0], ATTN_SLABS["v"][0], ATTN_SLABS["z"],
             ATTN_SLABS["q"][1], ATTN_SLABS["k"][1], ATTN_SLABS["v"][1],
             ATTN_SLABS["q"][2], ATTN_SLABS["k"][2], ATTN_SLABS["v"][2]]
    tk_max = min(4 * 64, seq_len // GROUP_PATTERNS[0][1])
    y = pl.pallas_call(
        functools.partial(_fused_attn_kernel, seq_len=seq_len),
        grid=(batch, N_HEAD_PAIRS),
        in_specs=[slab_spec(s) for s in order],
        out_specs=pl.BlockSpec((None, None, seq_len, LANES), lambda b, hp: (hp, b, 0, 0)),
        out_shape=jax.ShapeDtypeStruct((N_HEAD_PAIRS, batch, seq_len, LANES), BF16),
        scratch_shapes=[pltpu.VMEM((TILE, LANES), F32)] * 3 + [
            pltpu.VMEM((3, TQ, tk_max), F32),
            pltpu.VMEM((TILE // TQ, 2 * TQ, tk_max), BF16),
            pltpu.VMEM((TILE // TQ, TQ, LANES), F32)],
        compiler_params=_params("parallel", "parallel"),
        name="fused_banded_attn",
    )(*([view] * N_ATTN_SLABS))
    return y.reshape(N_HEAD_PAIRS, batch * seq_len, LANES)


def _attn_out_kernel(y_ref, x_ref, w_ref, out_ref):
    y = jnp.concatenate([y_ref[hp] for hp in range(N_HEAD_PAIRS)], axis=1)
    out_ref[...] = x_ref[...] + jnp.dot(y, w_ref[...], preferred_element_type=F32)


def _attn_out(y, x2d, w_out):
    ntok = x2d.shape[0]
    tm = 1024
    return pl.pallas_call(
        _attn_out_kernel,
        grid=(ntok // tm,),
        in_specs=[
            pl.BlockSpec((N_HEAD_PAIRS, tm, LANES), lambda i: (0, i, 0)),
            pl.BlockSpec((tm, D_MODEL), lambda i: (i, 0)),
            pl.BlockSpec((D_MODEL, D_MODEL), lambda i: (0, 0)),
        ],
        out_specs=pl.BlockSpec((tm, D_MODEL), lambda i: (i, 0)),
        out_shape=jax.ShapeDtypeStruct((ntok, D_MODEL), F32),
        compiler_params=_params("parallel"),
        name="attn_outproj",
    )(y, x2d, w_out)


def _ssm_inproj_kernel(x_ref, g_ref, wt_ref, o_ref, hn_ref, xs_ref):
    _stage_lane_blocks(x_ref[...], xs_ref)
    gain = g_ref[...]
    half = TILE // 2
    for part in range(2):
        for t in range(part * CHUNK // 2, (part + 1) * CHUNK // 2):
            rows = _strided_rows(xs_ref, t, LANES, CHUNK)
            hn_ref[t * LANES:(t + 1) * LANES, :] = _rmsnorm_rows(rows, gain).astype(BF16)
        cols = slice(part * half, (part + 1) * half)
        for j in range(2):
            o_ref[j, :, cols] = lax.dot_general(wt_ref[j * D_MODEL:(j + 1) * D_MODEL, :], hn_ref[cols, :],
                                                (((1,), (1,)), ((), ())), preferred_element_type=F32).astype(BF16)


def _ssm_inproj(x2d, gain, w_in_t):
    ntok = x2d.shape[0]
    return pl.pallas_call(
        _ssm_inproj_kernel,
        grid=(ntok // TILE,),
        in_specs=[
            pl.BlockSpec((TILE, D_MODEL), lambda i: (i, 0)),
            pl.BlockSpec((1, D_MODEL), lambda i: (0, 0)),
            pl.BlockSpec((2 * D_MODEL, D_MODEL), lambda i: (0, 0), pipeline_mode=pl.Buffered(1)),
        ],
        out_specs=pl.BlockSpec((2, D_MODEL, TILE), lambda i: (0, 0, i)),
        out_shape=jax.ShapeDtypeStruct((2, D_MODEL, ntok), BF16),
        scratch_shapes=[pltpu.VMEM((TILE, D_MODEL), BF16), pltpu.VMEM((LANE_BLOCKS, TILE, LANES), F32)],
        compiler_params=_params("parallel"),
        name="ssm_inproj",
    )(x2d, gain, w_in_t)


def _ssm_core_kernel(u_ref, mt_ref, w_ref, vt_ref, a16_ref, apl_ref, y_ref, ut_ref, s_ref, hf_ref, hb_ref,
                     *, n_seq, chunks_per_seq):
    n_tiles = n_seq * chunks_per_seq // LANES
    p = SSM_STATE
    sub = 8
    for tile in range(n_tiles):
        for t in range(CHUNK):
            src = (tile * CHUNK + t) * LANES
            ut_ref[t * SSM_GROUP:(t + 1) * SSM_GROUP, tile * LANES:(tile + 1) * LANES] = u_ref[:, src:src + LANES]
    ut = ut_ref[...]
    s_ref[...] = lax.dot_general(ut, w_ref[...], (((0,), (0,)), ((), ())), preferred_element_type=F32)

    lane = lax.broadcasted_iota(jnp.int32, (sub, LANES), 1)
    row = lax.broadcasted_iota(jnp.int32, (sub, LANES), 0)
    is_f = lane < p
    steps_in = jnp.where(is_f, row, sub - 1 - row)
    cur = (jnp.broadcast_to(a16_ref[0:1, :], (sub, LANES)), jnp.broadcast_to(a16_ref[1:2, :], (sub, LANES)))
    levels = []
    shift = 1
    while shift < sub:
        levels.append((shift, steps_in >= shift, cur))
        cur = _cmul(cur[0], cur[1], cur[0], cur[1])
        shift *= 2
    apl_re = apl_ref[0]
    apl_im = apl_ref[1]
    inner = steps_in >= 1
    n_steps = chunks_per_seq // sub

    def toward_scan_start(x, by):
        return jnp.where(is_f, pltpu.roll(x, by, axis=0), pltpu.roll(x, sub - by, axis=0))

    def seq_body(b, carry):
        seq0 = b * chunks_per_seq

        def step(i, h):
            h_re, h_im = h
            rf = pl.multiple_of(seq0 + i * sub, sub)
            rb = pl.multiple_of(seq0 + (n_steps - 1 - i) * sub, sub)
            x_re = jnp.where(is_f, s_ref[pl.ds(rf, sub), 0:LANES], s_ref[pl.ds(rb, sub), 0:LANES])
            x_im = jnp.where(is_f, s_ref[pl.ds(rf, sub), LANES:2 * LANES], s_ref[pl.ds(rb, sub), LANES:2 * LANES])
            for by, keep, (lr, li) in levels:
                d_re, d_im = _cmul(lr, li, jnp.where(keep, toward_scan_start(x_re, by), 0.0),
                                   jnp.where(keep, toward_scan_start(x_im, by), 0.0))
                x_re = x_re + d_re
                x_im = x_im + d_im
            c_re, c_im = _cmul(apl_re, apl_im, h_re, h_im)
            t_re = x_re + c_re
            t_im = x_im + c_im
            h0_re = jnp.where(inner, toward_scan_start(t_re, 1), h_re)
            h0_im = jnp.where(inner, toward_scan_start(t_im, 1), h_im)
            hf_ref[pl.ds(rf, sub), 0:LANES] = h0_re
            hf_ref[pl.ds(rf, sub), LANES:2 * LANES] = h0_im
            hb_ref[pl.ds(rb, sub), 0:LANES] = h0_re
            hb_ref[pl.ds(rb, sub), LANES:2 * LANES] = h0_im
            n_re = jnp.where(is_f[0:1], t_re[sub - 1:sub], t_re[0:1])
            n_im = jnp.where(is_f[0:1], t_im[sub - 1:sub], t_im[0:1])
            return n_re, n_im

        zero = jnp.zeros((1, LANES), F32)
        lax.fori_loop(0, n_steps, step, (zero, zero), unroll=8)
        return carry

    lax.fori_loop(0, n_seq, seq_body, 0)
    is_f_wide = lax.rem(lax.broadcasted_iota(jnp.int32, (1, 2 * LANES), 1), LANES) < p
    h0 = jnp.where(is_f_wide, hf_ref[...], hb_ref[...]).astype(BF16)
    yt = (jnp.dot(mt_ref[...], ut, preferred_element_type=F32)
          + lax.dot_general(vt_ref[...], h0, (((1,), (1,)), ((), ())), preferred_element_type=F32)).astype(BF16)
    for tile in range(n_tiles):
        for t in range(CHUNK):
            dst = (tile * CHUNK + t) * LANES
            y_ref[:, dst:dst + LANES] = yt[t * SSM_GROUP:(t + 1) * SSM_GROUP, tile * LANES:(tile + 1) * LANES]


def _ssm_core(uz_t, mats, batch, seq_len):
    m_t, w_n, v_t, a16, apl = mats
    ntok = batch * seq_len
    n_chunks = ntok // CHUNK
    mat_spec = pl.BlockSpec((None, CHUNK_COLS, CHUNK_COLS), lambda g: (g, 0, 0))
    return pl.pallas_call(
        functools.partial(_ssm_core_kernel, n_seq=batch, chunks_per_seq=seq_len // CHUNK),
        grid=(SSM_GROUPS,),
        in_specs=[pl.BlockSpec((None, SSM_GROUP, ntok), lambda g: (0, g, 0)),
                  mat_spec, mat_spec, mat_spec,
                  pl.BlockSpec((None, 2, LANES), lambda g: (g, 0, 0)),
                  pl.BlockSpec((None, 2, 8, LANES), lambda g: (g, 0, 0, 0))],
        out_specs=pl.BlockSpec((SSM_GROUP, ntok), lambda g: (g, 0)),
        out_shape=jax.ShapeDtypeStruct((D_MODEL, ntok), BF16),
        scratch_shapes=[pltpu.VMEM((CHUNK_COLS, n_chunks), BF16)] + [pltpu.VMEM((n_chunks, 2 * LANES), F32)] * 3,
        compiler_params=_params("parallel"),
        name="ssm_chunk_scan",
    )(uz_t, m_t, w_n, v_t, a16, apl)


def _ssm_glu_kernel(y_ref, z_ref, wg_ref, bg_ref, o_ref):
    gl = _gelu_tanh(y_ref[...].astype(F32))
    t = jnp.dot(wg_ref[...], gl.astype(BF16), preferred_element_type=F32) + bg_ref[...]
    y = gl * _sigmoid(t)
    o_ref[...] = (y * _silu(z_ref[...].astype(F32))).astype(BF16)


def _ssm_glu(y_t, uz_t, w_glu_t, b_glu_col):
    ntok = y_t.shape[1]
    tl = 1024
    return pl.pallas_call(
        _ssm_glu_kernel,
        grid=(ntok // tl,),
        in_specs=[pl.BlockSpec((D_MODEL, tl), lambda i: (0, i)),
                  pl.BlockSpec((None, D_MODEL, tl), lambda i: (1, 0, i)),
                  pl.BlockSpec((D_MODEL, D_MODEL), lambda i: (0, 0)),
                  pl.BlockSpec((D_MODEL, 1), lambda i: (0, 0))],
        out_specs=pl.BlockSpec((D_MODEL, tl), lambda i: (0, i)),
        out_shape=jax.ShapeDtypeStruct((D_MODEL, ntok), BF16),
        compiler_params=_params("parallel"),
        name="ssm_glu",
    )(y_t, uz_t, w_glu_t, b_glu_col)


def _ssm_out_kernel(y_ref, x_ref, w_ref, fg_ref, out_ref, ys_ref, *, final):
    half = TILE // 2
    for part in range(2):
        res = lax.dot_general(y_ref[:, part * half:(part + 1) * half], w_ref[...], (((0,), (0,)), ((), ())),
                              preferred_element_type=F32)
        for tt in range(CHUNK // 2):
            rows = pl.ds(part * (CHUNK // 2) + tt, LANES, stride=CHUNK)
            for c in range(LANE_BLOCKS):
                ys_ref[c, rows, :] = res[tt * LANES:(tt + 1) * LANES, c * LANES:(c + 1) * LANES]
    out = x_ref[...] + jnp.concatenate([ys_ref[c] for c in range(LANE_BLOCKS)], axis=1)
    if final:
        out = _rmsnorm_rows(out, fg_ref[...])
    out_ref[...] = out


def _ssm_out(y3_t, x2d, w_out, final_gain, final):
    ntok = x2d.shape[0]
    return pl.pallas_call(
        functools.partial(_ssm_out_kernel, final=final),
        grid=(ntok // TILE,),
        in_specs=[pl.BlockSpec((D_MODEL, TILE), lambda i: (0, i)),
                  pl.BlockSpec((TILE, D_MODEL), lambda i: (i, 0)),
                  pl.BlockSpec((D_MODEL, D_MODEL), lambda i: (0, 0), pipeline_mode=pl.Buffered(1)),
                  pl.BlockSpec((1, D_MODEL), lambda i: (0, 0))],
        out_specs=pl.BlockSpec((TILE, D_MODEL), lambda i: (i, 0)),
        out_shape=jax.ShapeDtypeStruct((ntok, D_MODEL), F32),
        scratch_shapes=[pltpu.VMEM((LANE_BLOCKS, TILE, LANES), F32)],
        compiler_params=_params("parallel"),
        name="ssm_outproj",
    )(y3_t, x2d, w_out, final_gain)


def _pair_heads(w):
    rows = w.shape[0]
    return w.reshape(rows, N_HEAD_PAIRS, 2, 2, HEAD_DIM // 2).transpose(0, 1, 3, 2, 4).reshape(rows, -1)


def _prep_attn_w_in(w_in):
    width = N_HEADS * HEAD_DIM
    blocks = [None] * N_ATTN_SLABS
    for g in range(len(GROUP_PATTERNS)):
        base = 3 * g * width
        blocks[ATTN_SLABS["q"][g]] = _pair_heads(w_in[:, base:base + width]) * (HEAD_DIM ** -0.5)
        blocks[ATTN_SLABS["k"][g]] = _pair_heads(w_in[:, base + width:base + 2 * width])
        blocks[ATTN_SLABS["v"][g]] = w_in[:, base + 2 * width:base + 3 * width]
    blocks[ATTN_SLABS["z"]] = w_in[:, 3 * len(GROUP_PATTERNS) * width:]
    return jnp.concatenate(blocks, axis=1).astype(BF16)


def _rope_tables(seq_len):
    inv_freq = ROPE_THETA ** (-jnp.arange(0, HEAD_DIM, 2, dtype=F32) / HEAD_DIM)
    ang = jnp.arange(seq_len, dtype=F32)[:, None] * inv_freq[None, :]
    cos, sin = jnp.cos(ang), jnp.sin(ang)
    cos_t = jnp.concatenate([cos] * 4, axis=1)
    sin_t = jnp.concatenate([-sin, -sin, sin, sin], axis=1)

    def orders(tab, identity):
        out = []
        for _, dilation in GROUP_PATTERNS:
            rpc = TILE // dilation
            out.append(tab.reshape(seq_len // TILE, rpc, dilation, LANES).transpose(0, 2, 1, 3).reshape(seq_len, LANES))
        out.append(jnp.full_like(tab, identity))
        return jnp.stack(out)

    return orders(cos_t, 1.0), orders(sin_t, 0.0)


def _prep_ssm(a_re, a_im, log_step, b_re, b_im, c_re, c_im, d):
    hi = lax.Precision.HIGHEST
    g_n, p_n, c_n, t_n = SSM_GROUPS, SSM_STATE, SSM_GROUP, CHUNK
    dt = jnp.exp(log_step)[..., None]
    ldr = a_re * dt
    ldi = a_im * dt
    abar_re = jnp.exp(ldr) * jnp.cos(ldi)
    abar_im = jnp.exp(ldr) * jnp.sin(ldi)
    den = a_re * a_re + a_im * a_im
    nr = abar_re - 1.0
    ni = abar_im
    f_re = (nr * a_re + ni * a_im) / den
    f_im = (ni * a_re - nr * a_im) / den
    bb_re = f_re[..., None] * b_re - f_im[..., None] * b_im
    bb_im = f_re[..., None] * b_im + f_im[..., None] * b_re
    n = jnp.arange(t_n + 1, dtype=F32)[:, None, None, None]
    pw_re = jnp.exp(n * ldr) * jnp.cos(n * ldi)
    pw_im = jnp.exp(n * ldr) * jnp.sin(n * ldi)
    ca_re = c_re[None] * pw_re[:, :, :, None, :] - c_im[None] * pw_im[:, :, :, None, :]
    ca_im = c_re[None] * pw_im[:, :, :, None, :] + c_im[None] * pw_re[:, :, :, None, :]
    bbt_re = bb_re.transpose(0, 1, 3, 2)
    bbt_im = bb_im.transpose(0, 1, 3, 2)
    kern = jnp.einsum('dgkp,ldgcp->dgklc', jnp.concatenate([bbt_re, bbt_im], axis=-1),
                      jnp.concatenate([ca_re, -ca_im], axis=-1), precision=hi)
    center = kern[0][:, :, 0, :] + kern[1][:, :, 0, :] + jnp.eye(c_n, dtype=F32)[None] * d.reshape(g_n, 1, c_n)
    taps = jnp.concatenate([kern[0][:, :, t_n - 1:0:-1, :], center[:, :, None, :], kern[1][:, :, 1:t_n, :]],
                           axis=2)
    taps = taps.transpose(0, 3, 2, 1).reshape(g_n, c_n, (2 * t_n - 1) * c_n)
    m_t = jnp.stack([taps[:, :, (t_n - 1 - t) * c_n:(t_n - 1 - t) * c_n + CHUNK_COLS] for t in range(t_n)], axis=1)
    m_t = m_t.reshape(g_n, CHUNK_COLS, CHUNK_COLS)

    def state_in(pw_r, pw_i, direction):
        pr = pw_r.transpose(1, 0, 2)[:, :, None, :]
        pi = pw_i.transpose(1, 0, 2)[:, :, None, :]
        br, bi = bbt_re[direction][:, None], bbt_im[direction][:, None]
        return ((pr * br - pi * bi).reshape(g_n, CHUNK_COLS, p_n), (pr * bi + pi * br).reshape(g_n, CHUNK_COLS, p_n))

    wf_re, wf_im = state_in(pw_re[t_n - 1::-1, 0], pw_im[t_n - 1::-1, 0], 0)
    wb_re, wb_im = state_in(pw_re[:t_n, 1], pw_im[:t_n, 1], 1)
    w_mat = jnp.concatenate([wf_re, wb_re, wf_im, wb_im], axis=2)

    def state_out_t(ca_r, ca_i):
        rows = lambda x: x.transpose(1, 0, 2, 3).reshape(g_n, CHUNK_COLS, p_n)
        return rows(ca_r), rows(-ca_i)

    vf_re, vf_im = state_out_t(ca_re[1:t_n + 1, 0], ca_im[1:t_n + 1, 0])
    vb_re, vb_im = state_out_t(ca_re[t_n:0:-1, 1], ca_im[t_n:0:-1, 1])
    v_t = jnp.concatenate([vf_re, vb_re, vf_im, vb_im], axis=2)

    lanes = lambda f, b: jnp.concatenate([f, b], axis=-1)
    a16 = jnp.stack([lanes(pw_re[t_n, 0], pw_re[t_n, 1]), lanes(pw_im[t_n, 0], pw_im[t_n, 1])], axis=1)
    j_row = jnp.arange(8, dtype=F32)[None, :, None]
    n_f = t_n * (j_row + 1.0)
    n_b = t_n * (8.0 - j_row)
    pow_rows = lambda nn, dr: (jnp.exp(nn * ldr[dr][:, None, :]) * jnp.cos(nn * ldi[dr][:, None, :]),
                               jnp.exp(nn * ldr[dr][:, None, :]) * jnp.sin(nn * ldi[dr][:, None, :]))
    plf_re, plf_im = pow_rows(n_f, 0)
    plb_re, plb_im = pow_rows(n_b, 1)
    apl = jnp.stack([lanes(plf_re, plb_re), lanes(plf_im, plb_im)], axis=1)
    return m_t.astype(BF16), w_mat.astype(BF16), v_t.astype(BF16), a16, apl


def _attention_layer(x2d, batch, seq_len, params, rope):
    gain, w_in_p, w_out = params
    qkvz = _attn_inproj(x2d, gain, w_in_p, rope[0], rope[1], seq_len)
    y = _fused_attention(qkvz, batch, seq_len)
    return _attn_out(y, x2d, w_out)


def _ssm_layer(x2d, batch, seq_len, params, final_gain, final):
    gain, w_in_t, mats, w_glu_t, b_glu_col, w_out = params
    uz_t = _ssm_inproj(x2d, gain, w_in_t)
    y_t = _ssm_core(uz_t, mats, batch, seq_len)
    y3_t = _ssm_glu(y_t, uz_t, w_glu_t, b_glu_col)
    return _ssm_out(y3_t, x2d, w_out, final_gain, final)


def _trunk(x, attn_params, ssm_params, final_gain):
    batch, seq_len, _ = x.shape
    assert seq_len % TILE == 0
    rope = _rope_tables(seq_len)
    x2d = x.reshape(batch * seq_len, D_MODEL)
    for i in range(DEPTH):
        j = i // 2
        if i % 2 == 0:
            x2d = _attention_layer(x2d, batch, seq_len, attn_params[j], rope)
        else:
            x2d = _ssm_layer(x2d, batch, seq_len, ssm_params[j], final_gain, final=(i == DEPTH - 1))
    return x2d.reshape(batch, seq_len, D_MODEL)


def _prepare(attn_norm, attn_w_in, attn_w_out, ssm_norm, ssm_w_in, ssm_a_re, ssm_a_im, ssm_log_step, ssm_b_re,
             ssm_b_im, ssm_c_re, ssm_c_im, ssm_d, ssm_w_glu, ssm_b_glu, ssm_w_out, final_norm):
    attn_params = []
    for j in range(attn_norm.shape[0]):
        attn_params.append((attn_norm[j][None, :], _prep_attn_w_in(attn_w_in[j]), attn_w_out[j].astype(BF16)))
    ssm_params = []
    for j in range(ssm_norm.shape[0]):
        mats = _prep_ssm(ssm_a_re[j], ssm_a_im[j], ssm_log_step[j], ssm_b_re[j], ssm_b_im[j],
                         ssm_c_re[j], ssm_c_im[j], ssm_d[j])
        ssm_params.append((ssm_norm[j][None, :], ssm_w_in[j].T.astype(BF16), mats,
                           ssm_w_glu[j].T.astype(BF16), ssm_b_glu[j][:, None], ssm_w_out[j].astype(BF16)))
    return attn_params, ssm_params, final_norm[None, :]


def kernel(x_prompt, x_sample, attn_norm, attn_w_in, attn_w_out, ssm_norm, ssm_w_in, ssm_a_re, ssm_a_im, ssm_log_step, ssm_b_re, ssm_b_im, ssm_c_re, ssm_c_im, ssm_d, ssm_w_glu, ssm_b_glu, ssm_w_out, final_norm):
    attn_params, ssm_params, final_gain = _prepare(
        attn_norm, attn_w_in, attn_w_out, ssm_norm, ssm_w_in, ssm_a_re, ssm_a_im, ssm_log_step, ssm_b_re,
        ssm_b_im, ssm_c_re, ssm_c_im, ssm_d, ssm_w_glu, ssm_b_glu, ssm_w_out, final_norm)
    y_prompt = _trunk(x_prompt, attn_params, ssm_params, final_gain)
    y_sample = _trunk(x_sample, attn_params, ssm_params, final_gain)
    return (y_prompt, y_sample)
```

```python
import functools
import math

import jax
import jax.numpy as jnp
from jax import lax
from jax.experimental import pallas as pl
from jax.experimental.pallas import tpu as pltpu

F32 = jnp.float32
BF16 = jnp.bfloat16

D_MODEL = 1024
DEPTH = 4
GROUP_PATTERNS = ((128, 1), (512, 4), (2048, 16))
N_HEADS = 16
HEAD_DIM = 64
N_HEAD_PAIRS = N_HEADS // 2
LANES = 128
LANE_BLOCKS = D_MODEL // LANES
ROPE_THETA = 10000.0
SSM_GROUP = 16
SSM_GROUPS = D_MODEL // SSM_GROUP
SSM_STATE = 64
CHUNK = 16
CHUNK_COLS = CHUNK * SSM_GROUP
TILE = CHUNK * LANES
TQ = 128
NORM_EPS = 1e-6
VMEM_LIMIT = 56 * 1024 * 1024

ATTN_SLABS = {"q": (0, 4, 7), "k": (1, 5, 8), "v": (2, 6, 9), "z": 3}
N_ATTN_SLABS = 10


def _sigmoid(x):
    return 1.0 / (1.0 + jnp.exp(-x))


def _silu(x):
    return x * _sigmoid(x)


def _gelu_tanh(x):
    c = math.sqrt(2.0 / math.pi)
    return x * (0.5 * (1.0 + jnp.tanh(c * (x + 0.044715 * (x * x * x)))))


def _rmsnorm_rows(xf, gain):
    return xf * lax.rsqrt(jnp.mean(xf * xf, axis=-1, keepdims=True) + NORM_EPS) * gain


def _aligned(x, multiple):
    return x if isinstance(x, int) else pl.multiple_of(x, multiple)


def _clip(x, lo, hi):
    return min(max(x, lo), hi) if isinstance(x, int) else jnp.clip(x, lo, hi)


def _cmul(ar, ai, br, bi):
    return ar * br - ai * bi, ar * bi + ai * br


def _params(*semantics):
    return pltpu.CompilerParams(dimension_semantics=semantics, vmem_limit_bytes=VMEM_LIMIT)


def _stage_lane_blocks(x, slab_ref):
    for c in range(LANE_BLOCKS):
        slab_ref[c] = x[:, c * LANES:(c + 1) * LANES]


def _strided_rows(slab_ref, start, count, stride):
    return jnp.concatenate(
        [slab_ref[c, pl.ds(start, count, stride=stride), :] for c in range(LANE_BLOCKS)], axis=1)


def _attn_inproj_kernel(x_ref, g_ref, w_ref, cos_ref, sin_ref, o_ref, hn_ref, xs_ref):
    j = pl.program_id(1)

    @pl.when(j == 0)
    def _():
        hn = _rmsnorm_rows(x_ref[...], g_ref[...])
        _stage_lane_blocks(hn, xs_ref)
        hn_ref[...] = hn.astype(BF16)

    for g in (1, 2):
        @pl.when(j == ATTN_SLABS["q"][g])
        def _(dilation=GROUP_PATTERNS[g][1]):
            rpc = TILE // dilation
            for r in range(dilation):
                hn_ref[r * rpc:(r + 1) * rpc, :] = _strided_rows(xs_ref, r, rpc, dilation).astype(BF16)

    half = TILE // 2
    for part in range(2):
        rows = slice(part * half, (part + 1) * half)
        acc = jnp.dot(hn_ref[rows, :], w_ref[...], preferred_element_type=F32)
        c = cos_ref[rows, :]
        s = sin_ref[rows, :]
        for hp in range(N_HEAD_PAIRS):
            t = acc[:, hp * LANES:(hp + 1) * LANES]
            o_ref[hp, rows, :] = (t * c + pltpu.roll(t, 64, axis=1) * s).astype(BF16)


def _attn_inproj(x2d, gain, w_p, cos_t, sin_t, seq_len):
    ntok = x2d.shape[0]
    assert seq_len % TILE == 0
    pos_blocks = seq_len // TILE

    def table_map(i, j):
        order = jnp.where(j >= ATTN_SLABS["q"][2], 2, jnp.where(j >= ATTN_SLABS["q"][1], 1, 0))
        is_plain = functools.reduce(jnp.logical_or, [j == b for b in ATTN_SLABS["v"] + (ATTN_SLABS["z"],)])
        return (jnp.where(is_plain, len(GROUP_PATTERNS), order), i % pos_blocks, 0)

    return pl.pallas_call(
        _attn_inproj_kernel,
        grid=(ntok // TILE, N_ATTN_SLABS),
        in_specs=[
            pl.BlockSpec((TILE, D_MODEL), lambda i, j: (i, 0)),
            pl.BlockSpec((1, D_MODEL), lambda i, j: (0, 0)),
            pl.BlockSpec((D_MODEL, D_MODEL), lambda i, j: (0, j)),
            pl.BlockSpec((None, TILE, LANES), table_map),
            pl.BlockSpec((None, TILE, LANES), table_map),
        ],
        out_specs=pl.BlockSpec((None, N_HEAD_PAIRS, TILE, LANES), lambda i, j: (j, 0, i, 0)),
        out_shape=jax.ShapeDtypeStruct((N_ATTN_SLABS, N_HEAD_PAIRS, ntok, LANES), BF16),
        scratch_shapes=[pltpu.VMEM((TILE, D_MODEL), BF16), pltpu.VMEM((LANE_BLOCKS, TILE, LANES), F32)],
        compiler_params=_params("parallel", "arbitrary"),
        name="attn_inproj",
    )(x2d, gain, w_p, cos_t, sin_t)


def _fused_attn_kernel(q0, k0, v0, z_ref, q1, k1, v1, q2, k2, v2, y_ref, acc_ref, m_ref, l_ref, bias_ref,
                       p_all, m_all, *, seq_len):
    lane = lax.broadcasted_iota(jnp.int32, (1, LANES), 1)
    head_of_lane = lax.rem(lane, 64) // 32
    first_head_out = lane < HEAD_DIM
    groups = ((q0, k0, v0), (q1, k1, v1), (q2, k2, v2))

    tk_max = bias_ref.shape[2]
    row_i = lax.broadcasted_iota(jnp.int32, (TQ, tk_max), 0)
    col_i = lax.broadcasted_iota(jnp.int32, (TQ, tk_max), 1)
    for o in range(bias_ref.shape[0]):
        bias_ref[o] = jnp.where(jnp.abs(row_i - col_i + o * 64) <= 64, 0.0, -jnp.inf).astype(F32)

    n_windows = seq_len // TILE

    def window(w, carry):
        base = _aligned(w * TILE, TILE)
        for g, (q_ref, k_ref, v_ref) in enumerate(groups):
            win, dilation = GROUP_PATTERNS[g]
            half = win // (2 * dilation)
            length = seq_len // dilation
            rpc = TILE // dilation
            nqb = rpc // TQ
            tk = min(4 * half, length)
            n_pieces = tk // half
            n_blocks = dilation * nqb

            def locate(idx, dilation=dilation, half=half, length=length, rpc=rpc, nqb=nqb, tk=tk):
                r = idx // nqb
                lw = (idx % nqb) * TQ
                l0 = w * rpc + lw
                ks = _clip(l0 - half, 0, length - tk)
                return r, lw, l0, ks

            def key_rows(ref, r, ks, rpc=rpc, half=half, n_pieces=n_pieces):
                parts = []
                for p in range(n_pieces):
                    l = ks + p * half
                    row = _aligned((l // rpc) * TILE + r * rpc + l % rpc, half)
                    parts.append(ref[pl.ds(row, half), :])
                return jnp.concatenate(parts, axis=0)

            def probabilities(idx, q_ref=q_ref, k_ref=k_ref, rpc=rpc, half=half, tk=tk):
                r, lw, l0, ks = locate(idx)
                q = q_ref[pl.ds(_aligned(base + r * rpc + lw, TQ), TQ), :]
                k = key_rows(k_ref, r, ks)
                bias = bias_ref[(l0 - ks) // half][:, 0:tk]
                maxima = []
                for h in range(2):
                    q_h = jnp.where(head_of_lane == h, q, jnp.zeros_like(q))
                    s = lax.dot_general(q_h, k, (((1,), (1,)), ((), ())), preferred_element_type=F32) + bias
                    m = jnp.max(s, axis=-1, keepdims=True)
                    p_all[idx, h * TQ:(h + 1) * TQ, 0:tk] = jnp.exp(s - m).astype(BF16)
                    maxima.append(m)
                m_all[idx] = jnp.where(first_head_out, maxima[0], maxima[1])

            def values(idx, g=g, v_ref=v_ref, dilation=dilation, tk=tk):
                r, lw, l0, ks = locate(idx)
                v = key_rows(v_ref, r, ks)
                v_aug = jnp.concatenate([v, jnp.ones_like(v)], axis=1)
                pv2 = jnp.dot(p_all[idx, :, 0:tk], v_aug, preferred_element_type=F32)
                pv = jnp.where(first_head_out, pv2[:TQ, :LANES], pv2[TQ:, :LANES])
                dd = jnp.where(first_head_out, pv2[:TQ, LANES:], pv2[TQ:, LANES:])
                mm = m_all[idx]
                if dilation == 1:
                    rows = pl.ds(_aligned(lw, TQ), TQ)
                else:
                    rows = pl.ds(lw * dilation + r, TQ, stride=dilation)
                if g == 0:
                    acc_ref[rows, :] = pv
                    m_ref[rows, :] = mm
                    l_ref[rows, :] = dd
                else:
                    m_old = m_ref[rows, :]
                    m_new = jnp.maximum(m_old, mm)
                    e_old = jnp.exp(m_old - m_new)
                    e_new = jnp.exp(mm - m_new)
                    acc_ref[rows, :] = acc_ref[rows, :] * e_old + pv * e_new
                    l_ref[rows, :] = l_ref[rows, :] * e_old + dd * e_new
                    if g + 1 < len(groups):
                        m_ref[rows, :] = m_new

            for phase in (probabilities, values):
                for idx in range(n_blocks):
                    phase(idx)
        gate = _silu(z_ref[pl.ds(base, TILE), :].astype(F32))
        y_ref[pl.ds(base, TILE), :] = (acc_ref[...] / l_ref[...] * gate).astype(BF16)
        return carry

    if n_windows == 1:
        window(0, 0)
    else:
        lax.fori_loop(0, n_windows, window, 0)


def _fused_attention(qkvz, batch, seq_len):
    view = qkvz.reshape(N_ATTN_SLABS, N_HEAD_PAIRS, batch, seq_len, LANES)
    slab_bytes = seq_len * LANES * 2
    scratch_bytes = (3 * TILE * LANES + 3 * TQ * 4 * 64 + TILE * LANES) * 4 + TILE * 2 * 4 * 64 * 2
    double_buffered = 2 * (N_ATTN_SLABS + 1) * slab_bytes + scratch_bytes
    mode = {} if double_buffered <= VMEM_LIMIT - 4 * 1024 * 1024 else {"pipeline_mode": pl.Buffered(1)}

    def slab_spec(slab):
        return pl.BlockSpec((None, None, None, seq_len, LANES), lambda b, hp: (slab, hp, b, 0, 0), **mode)

    order = [ATTN_SLABS["q"][0], ATTN_SLABS["k"][0], ATTN_SLABS["v"][0], ATTN_SLABS["z"],
             ATTN_SLABS["q"][1], ATTN_SLABS["k"][1], ATTN_SLABS["v"][1],
             ATTN_SLABS["q"][2], ATTN_SLABS["k"][2], ATTN_SLABS["v"][2]]
    tk_max = min(4 * 64, seq_len // GROUP_PATTERNS[0][1])
    y = pl.pallas_call(
        functools.partial(_fused_attn_kernel, seq_len=seq_len),
        grid=(batch, N_HEAD_PAIRS),
        in_specs=[slab_spec(s) for s in order],
        out_specs=pl.BlockSpec((None, None, seq_len, LANES), lambda b, hp: (hp, b, 0, 0)),
        out_shape=jax.ShapeDtypeStruct((N_HEAD_PAIRS, batch, seq_len, LANES), BF16),
        scratch_shapes=[pltpu.VMEM((TILE, LANES), F32)] * 3 + [
            pltpu.VMEM((3, TQ, tk_max), F32),
            pltpu.VMEM((TILE // TQ, 2 * TQ, tk_max), BF16),
            pltpu.VMEM((TILE // TQ, TQ, LANES), F32)],
        compiler_params=_params("parallel", "parallel"),
        name="fused_banded_attn",
    )(*([view] * N_ATTN_SLABS))
    return y.reshape(N_HEAD_PAIRS, batch * seq_len, LANES)


def _attn_out_kernel(y_ref, x_ref, w_ref, out_ref):
    y = jnp.concatenate([y_ref[hp] for hp in range(N_HEAD_PAIRS)], axis=1)
    out_ref[...] = x_ref[...] + jnp.dot(y, w_ref[...], preferred_element_type=F32)


def _attn_out(y, x2d, w_out):
    ntok = x2d.shape[0]
    tm = 1024
    return pl.pallas_call(
        _attn_out_kernel,
        grid=(ntok // tm,),
        in_specs=[
            pl.BlockSpec((N_HEAD_PAIRS, tm, LANES), lambda i: (0, i, 0)),
            pl.BlockSpec((tm, D_MODEL), lambda i: (i, 0)),
            pl.BlockSpec((D_MODEL, D_MODEL), lambda i: (0, 0)),
        ],
        out_specs=pl.BlockSpec((tm, D_MODEL), lambda i: (i, 0)),
        out_shape=jax.ShapeDtypeStruct((ntok, D_MODEL), F32),
        compiler_params=_params("parallel"),
        name="attn_outproj",
    )(y, x2d, w_out)


def _ssm_inproj_kernel(x_ref, g_ref, wt_ref, o_ref, hn_ref, xs_ref):
    _stage_lane_blocks(x_ref[...], xs_ref)
    gain = g_ref[...]
    half = TILE // 2
    for part in range(2):
        for t in range(part * CHUNK // 2, (part + 1) * CHUNK // 2):
            rows = _strided_rows(xs_ref, t, LANES, CHUNK)
            hn_ref[t * LANES:(t + 1) * LANES, :] = _rmsnorm_rows(rows, gain).astype(BF16)
        cols = slice(part * half, (part + 1) * half)
        for j in range(2):
            o_ref[j, :, cols] = lax.dot_general(wt_ref[j * D_MODEL:(j + 1) * D_MODEL, :], hn_ref[cols, :],
                                                (((1,), (1,)), ((), ())), preferred_element_type=F32).astype(BF16)


def _ssm_inproj(x2d, gain, w_in_t):
    ntok = x2d.shape[0]
    return pl.pallas_call(
        _ssm_inproj_kernel,
        grid=(ntok // TILE,),
        in_specs=[
            pl.BlockSpec((TILE, D_MODEL), lambda i: (i, 0)),
            pl.BlockSpec((1, D_MODEL), lambda i: (0, 0)),
            pl.BlockSpec((2 * D_MODEL, D_MODEL), lambda i: (0, 0), pipeline_mode=pl.Buffered(1)),
        ],
        out_specs=pl.BlockSpec((2, D_MODEL, TILE), lambda i: (0, 0, i)),
        out_shape=jax.ShapeDtypeStruct((2, D_MODEL, ntok), BF16),
        scratch_shapes=[pltpu.VMEM((TILE, D_MODEL), BF16), pltpu.VMEM((LANE_BLOCKS, TILE, LANES), F32)],
        compiler_params=_params("parallel"),
        name="ssm_inproj",
    )(x2d, gain, w_in_t)


def _ssm_core_kernel(u_ref, mt_ref, w_ref, vt_ref, a16_ref, apl_ref, y_ref, ut_ref, s_ref, hf_ref, hb_ref,
                     *, n_seq, chunks_per_seq):
    n_tiles = n_seq * chunks_per_seq // LANES
    p = SSM_STATE
    sub = 8
    for tile in range(n_tiles):
        for t in range(CHUNK):
            src = (tile * CHUNK + t) * LANES
            ut_ref[t * SSM_GROUP:(t + 1) * SSM_GROUP, tile * LANES:(tile + 1) * LANES] = u_ref[:, src:src + LANES]
    ut = ut_ref[...]
    s_ref[...] = lax.dot_general(ut, w_ref[...], (((0,), (0,)), ((), ())), preferred_element_type=F32)

    lane = lax.broadcasted_iota(jnp.int32, (sub, LANES), 1)
    row = lax.broadcasted_iota(jnp.int32, (sub, LANES), 0)
    is_f = lane < p
    steps_in = jnp.where(is_f, row, sub - 1 - row)
    cur = (jnp.broadcast_to(a16_ref[0:1, :], (sub, LANES)), jnp.broadcast_to(a16_ref[1:2, :], (sub, LANES)))
    levels = []
    shift = 1
    while shift < sub:
        levels.append((shift, steps_in >= shift, cur))
        cur = _cmul(cur[0], cur[1], cur[0], cur[1])
        shift *= 2
    apl_re = apl_ref[0]
    apl_im = apl_ref[1]
    inner = steps_in >= 1
    n_steps = chunks_per_seq // sub

    def toward_scan_start(x, by):
        return jnp.where(is_f, pltpu.roll(x, by, axis=0), pltpu.roll(x, sub - by, axis=0))

    def seq_body(b, carry):
        seq0 = b * chunks_per_seq

        def step(i, h):
            h_re, h_im = h
            rf = pl.multiple_of(seq0 + i * sub, sub)
            rb = pl.multiple_of(seq0 + (n_steps - 1 - i) * sub, sub)
            x_re = jnp.where(is_f, s_ref[pl.ds(rf, sub), 0:LANES], s_ref[pl.ds(rb, sub), 0:LANES])
            x_im = jnp.where(is_f, s_ref[pl.ds(rf, sub), LANES:2 * LANES], s_ref[pl.ds(rb, sub), LANES:2 * LANES])
            for by, keep, (lr, li) in levels:
                d_re, d_im = _cmul(lr, li, jnp.where(keep, toward_scan_start(x_re, by), 0.0),
                                   jnp.where(keep, toward_scan_start(x_im, by), 0.0))
                x_re = x_re + d_re
                x_im = x_im + d_im
            c_re, c_im = _cmul(apl_re, apl_im, h_re, h_im)
            t_re = x_re + c_re
            t_im = x_im + c_im
            h0_re = jnp.where(inner, toward_scan_start(t_re, 1), h_re)
            h0_im = jnp.where(inner, toward_scan_start(t_im, 1), h_im)
            hf_ref[pl.ds(rf, sub), 0:LANES] = h0_re
            hf_ref[pl.ds(rf, sub), LANES:2 * LANES] = h0_im
            hb_ref[pl.ds(rb, sub), 0:LANES] = h0_re
            hb_ref[pl.ds(rb, sub), LANES:2 * LANES] = h0_im
            n_re = jnp.where(is_f[0:1], t_re[sub - 1:sub], t_re[0:1])
            n_im = jnp.where(is_f[0:1], t_im[sub - 1:sub], t_im[0:1])
            return n_re, n_im

        zero = jnp.zeros((1, LANES), F32)
        lax.fori_loop(0, n_steps, step, (zero, zero), unroll=8)
        return carry

    lax.fori_loop(0, n_seq, seq_body, 0)
    is_f_wide = lax.rem(lax.broadcasted_iota(jnp.int32, (1, 2 * LANES), 1), LANES) < p
    h0 = jnp.where(is_f_wide, hf_ref[...], hb_ref[...]).astype(BF16)
    yt = (jnp.dot(mt_ref[...], ut, preferred_element_type=F32)
          + lax.dot_general(vt_ref[...], h0, (((1,), (1,)), ((), ())), preferred_element_type=F32)).astype(BF16)
    for tile in range(n_tiles):
        for t in range(CHUNK):
            dst = (tile * CHUNK + t) * LANES
            y_ref[:, dst:dst + LANES] = yt[t * SSM_GROUP:(t + 1) * SSM_GROUP, tile * LANES:(tile + 1) * LANES]


def _ssm_core(uz_t, mats, batch, seq_len):
    m_t, w_n, v_t, a16, apl = mats
    ntok = batch * seq_len
    n_chunks = ntok // CHUNK
    mat_spec = pl.BlockSpec((None, CHUNK_COLS, CHUNK_COLS), lambda g: (g, 0, 0))
    return pl.pallas_call(
        functools.partial(_ssm_core_kernel, n_seq=batch, chunks_per_seq=seq_len // CHUNK),
        grid=(SSM_GROUPS,),
        in_specs=[pl.BlockSpec((None, SSM_GROUP, ntok), lambda g: (0, g, 0)),
                  mat_spec, mat_spec, mat_spec,
                  pl.BlockSpec((None, 2, LANES), lambda g: (g, 0, 0)),
                  pl.BlockSpec((None, 2, 8, LANES), lambda g: (g, 0, 0, 0))],
        out_specs=pl.BlockSpec((SSM_GROUP, ntok), lambda g: (g, 0)),
        out_shape=jax.ShapeDtypeStruct((D_MODEL, ntok), BF16),
        scratch_shapes=[pltpu.VMEM((CHUNK_COLS, n_chunks), BF16)] + [pltpu.VMEM((n_chunks, 2 * LANES), F32)] * 3,
        compiler_params=_params("parallel"),
        name="ssm_chunk_scan",
    )(uz_t, m_t, w_n, v_t, a16, apl)


def _ssm_glu_kernel(y_ref, z_ref, wg_ref, bg_ref, o_ref):
    gl = _gelu_tanh(y_ref[...].astype(F32))
    t = jnp.dot(wg_ref[...], gl.astype(BF16), preferred_element_type=F32) + bg_ref[...]
    y = gl * _sigmoid(t)
    o_ref[...] = (y * _silu(z_ref[...].astype(F32))).astype(BF16)


def _ssm_glu(y_t, uz_t, w_glu_t, b_glu_col):
    ntok = y_t.shape[1]
    tl = 1024
    return pl.pallas_call(
        _ssm_glu_kernel,
        grid=(ntok // tl,),
        in_specs=[pl.BlockSpec((D_MODEL, tl), lambda i: (0, i)),
                  pl.BlockSpec((None, D_MODEL, tl), lambda i: (1, 0, i)),
                  pl.BlockSpec((D_MODEL, D_MODEL), lambda i: (0, 0)),
                  pl.BlockSpec((D_MODEL, 1), lambda i: (0, 0))],
        out_specs=pl.BlockSpec((D_MODEL, tl), lambda i: (0, i)),
        out_shape=jax.ShapeDtypeStruct((D_MODEL, ntok), BF16),
        compiler_params=_params("parallel"),
        name="ssm_glu",
    )(y_t, uz_t, w_glu_t, b_glu_col)


def _ssm_out_kernel(y_ref, x_ref, w_ref, fg_ref, out_ref, ys_ref, *, final):
    half = TILE // 2
    for part in range(2):
        res = lax.dot_general(y_ref[:, part * half:(part + 1) * half], w_ref[...], (((0,), (0,)), ((), ())),
                              preferred_element_type=F32)
        for tt in range(CHUNK // 2):
            rows = pl.ds(part * (CHUNK // 2) + tt, LANES, stride=CHUNK)
            for c in range(LANE_BLOCKS):
                ys_ref[c, rows, :] = res[tt * LANES:(tt + 1) * LANES, c * LANES:(c + 1) * LANES]
    out = x_ref[...] + jnp.concatenate([ys_ref[c] for c in range(LANE_BLOCKS)], axis=1)
    if final:
        out = _rmsnorm_rows(out, fg_ref[...])
    out_ref[...] = out


def _ssm_out(y3_t, x2d, w_out, final_gain, final):
    ntok = x2d.shape[0]
    return pl.pallas_call(
        functools.partial(_ssm_out_kernel, final=final),
        grid=(ntok // TILE,),
        in_specs=[pl.BlockSpec((D_MODEL, TILE), lambda i: (0, i)),
                  pl.BlockSpec((TILE, D_MODEL), lambda i: (i, 0)),
                  pl.BlockSpec((D_MODEL, D_MODEL), lambda i: (0, 0), pipeline_mode=pl.Buffered(1)),
                  pl.BlockSpec((1, D_MODEL), lambda i: (0, 0))],
        out_specs=pl.BlockSpec((TILE, D_MODEL), lambda i: (i, 0)),
        out_shape=jax.ShapeDtypeStruct((ntok, D_MODEL), F32),
        scratch_shapes=[pltpu.VMEM((LANE_BLOCKS, TILE, LANES), F32)],
        compiler_params=_params("parallel"),
        name="ssm_outproj",
    )(y3_t, x2d, w_out, final_gain)


def _pair_heads(w_bf16, scale):
    new = jnp.arange(N_HEADS * HEAD_DIM)
    i = new % LANES
    old = (2 * (new // LANES) + (i % 64) // 32) * HEAD_DIM + (i // 64) * 32 + i % 32
    select = jnp.where(new[:, None] == old[None, :], scale, 0.0).astype(BF16)
    return jnp.dot(w_bf16, select, preferred_element_type=F32).astype(BF16)


def _prep_attn_w_in(w_in):
    width = N_HEADS * HEAD_DIM
    w_bf16 = w_in.astype(BF16)
    blocks = [None] * N_ATTN_SLABS
    for g in range(len(GROUP_PATTERNS)):
        base = 3 * g * width
        blocks[ATTN_SLABS["q"][g]] = _pair_heads(w_bf16[:, base:base + width], HEAD_DIM ** -0.5)
        blocks[ATTN_SLABS["k"][g]] = _pair_heads(w_bf16[:, base + width:base + 2 * width], 1.0)
        blocks[ATTN_SLABS["v"][g]] = w_bf16[:, base + 2 * width:base + 3 * width]
    blocks[ATTN_SLABS["z"]] = w_bf16[:, 3 * len(GROUP_PATTERNS) * width:]
    return jnp.concatenate(blocks, axis=1)


def _rope_tables(seq_len):
    inv_freq = ROPE_THETA ** (-jnp.arange(0, HEAD_DIM, 2, dtype=F32) / HEAD_DIM)
    ang = jnp.arange(seq_len, dtype=F32)[:, None] * inv_freq[None, :]
    cos, sin = jnp.cos(ang), jnp.sin(ang)
    cos_t = jnp.concatenate([cos] * 4, axis=1)
    sin_t = jnp.concatenate([-sin, -sin, sin, sin], axis=1)

    def orders(tab, identity):
        out = []
        for _, dilation in GROUP_PATTERNS:
            rpc = TILE // dilation
            out.append(tab.reshape(seq_len // TILE, rpc, dilation, LANES).transpose(0, 2, 1, 3).reshape(seq_len, LANES))
        out.append(jnp.full_like(tab, identity))
        return jnp.stack(out)

    return orders(cos_t, 1.0), orders(sin_t, 0.0)


def _prep_ssm(a_re, a_im, log_step, b_re, b_im, c_re, c_im, d):
    hi = lax.Precision.HIGHEST
    g_n, p_n, c_n, t_n = SSM_GROUPS, SSM_STATE, SSM_GROUP, CHUNK
    dt = jnp.exp(log_step)[..., None]
    ldr = a_re * dt
    ldi = a_im * dt
    abar_re = jnp.exp(ldr) * jnp.cos(ldi)
    abar_im = jnp.exp(ldr) * jnp.sin(ldi)
    den = a_re * a_re + a_im * a_im
    nr = abar_re - 1.0
    ni = abar_im
    f_re = (nr * a_re + ni * a_im) / den
    f_im = (ni * a_re - nr * a_im) / den
    bb_re = f_re[..., None] * b_re - f_im[..., None] * b_im
    bb_im = f_re[..., None] * b_im + f_im[..., None] * b_re
    n = jnp.arange(t_n + 1, dtype=F32)[:, None, None, None]
    pw_re = jnp.exp(n * ldr) * jnp.cos(n * ldi)
    pw_im = jnp.exp(n * ldr) * jnp.sin(n * ldi)
    ca_re = c_re[None] * pw_re[:, :, :, None, :] - c_im[None] * pw_im[:, :, :, None, :]
    ca_im = c_re[None] * pw_im[:, :, :, None, :] + c_im[None] * pw_re[:, :, :, None, :]
    kern = jnp.einsum('ldgcp,dgpk->ldgck', jnp.concatenate([ca_re, -ca_im], axis=-1),
                      jnp.concatenate([bb_re, bb_im], axis=2), precision=hi)
    kf, kb = kern[:t_n, 0], kern[:t_n, 1]
    center = kf[0] + kb[0] + jnp.eye(c_n, dtype=F32)[None] * d.reshape(g_n, c_n)[:, :, None]
    taps = jnp.concatenate([kf[:0:-1], center[None], kb[1:]], axis=0)
    taps = taps.transpose(1, 2, 0, 3).reshape(g_n, c_n, (2 * t_n - 1) * c_n)
    m_t = jnp.stack([taps[:, :, (t_n - 1 - t) * c_n:(t_n - 1 - t) * c_n + CHUNK_COLS] for t in range(t_n)], axis=1)
    m_t = m_t.reshape(g_n, CHUNK_COLS, CHUNK_COLS)

    bbt_re = bb_re.transpose(0, 1, 3, 2)
    bbt_im = bb_im.transpose(0, 1, 3, 2)

    def state_in(pw_r, pw_i, direction):
        pr = pw_r.transpose(1, 0, 2)[:, :, None, :]
        pi = pw_i.transpose(1, 0, 2)[:, :, None, :]
        br, bi = bbt_re[direction][:, None], bbt_im[direction][:, None]
        return ((pr * br - pi * bi).reshape(g_n, CHUNK_COLS, p_n), (pr * bi + pi * br).reshape(g_n, CHUNK_COLS, p_n))

    wf_re, wf_im = state_in(pw_re[t_n - 1::-1, 0], pw_im[t_n - 1::-1, 0], 0)
    wb_re, wb_im = state_in(pw_re[:t_n, 1], pw_im[:t_n, 1], 1)
    w_mat = jnp.concatenate([wf_re, wb_re, wf_im, wb_im], axis=2)

    def state_out_t(ca_r, ca_i):
        rows = lambda x: x.transpose(1, 0, 2, 3).reshape(g_n, CHUNK_COLS, p_n)
        return rows(ca_r), rows(-ca_i)

    vf_re, vf_im = state_out_t(ca_re[1:t_n + 1, 0], ca_im[1:t_n + 1, 0])
    vb_re, vb_im = state_out_t(ca_re[t_n:0:-1, 1], ca_im[t_n:0:-1, 1])
    v_t = jnp.concatenate([vf_re, vb_re, vf_im, vb_im], axis=2)

    lanes = lambda f, b: jnp.concatenate([f, b], axis=-1)
    a16 = jnp.stack([lanes(pw_re[t_n, 0], pw_re[t_n, 1]), lanes(pw_im[t_n, 0], pw_im[t_n, 1])], axis=1)
    j_row = jnp.arange(8, dtype=F32)[None, :, None]
    n_f = t_n * (j_row + 1.0)
    n_b = t_n * (8.0 - j_row)
    pow_rows = lambda nn, dr: (jnp.exp(nn * ldr[dr][:, None, :]) * jnp.cos(nn * ldi[dr][:, None, :]),
                               jnp.exp(nn * ldr[dr][:, None, :]) * jnp.sin(nn * ldi[dr][:, None, :]))
    plf_re, plf_im = pow_rows(n_f, 0)
    plb_re, plb_im = pow_rows(n_b, 1)
    apl = jnp.stack([lanes(plf_re, plb_re), lanes(plf_im, plb_im)], axis=1)
    return m_t.astype(BF16), w_mat.astype(BF16), v_t.astype(BF16), a16, apl


def _attention_layer(x2d, batch, seq_len, params, rope):
    gain, w_in_p, w_out = params
    qkvz = _attn_inproj(x2d, gain, w_in_p, rope[0], rope[1], seq_len)
    y = _fused_attention(qkvz, batch, seq_len)
    return _attn_out(y, x2d, w_out)


def _ssm_layer(x2d, batch, seq_len, params, final_gain, final):
    gain, w_in_t, mats, w_glu_t, b_glu_col, w_out = params
    uz_t = _ssm_inproj(x2d, gain, w_in_t)
    y_t = _ssm_core(uz_t, mats, batch, seq_len)
    y3_t = _ssm_glu(y_t, uz_t, w_glu_t, b_glu_col)
    return _ssm_out(y3_t, x2d, w_out, final_gain, final)


def _trunk(x, attn_params, ssm_params, final_gain):
    batch, seq_len, _ = x.shape
    assert seq_len % TILE == 0
    rope = _rope_tables(seq_len)
    x2d = x.reshape(batch * seq_len, D_MODEL)
    for i in range(DEPTH):
        j = i // 2
        if i % 2 == 0:
            x2d = _attention_layer(x2d, batch, seq_len, attn_params[j], rope)
        else:
            x2d = _ssm_layer(x2d, batch, seq_len, ssm_params[j], final_gain, final=(i == DEPTH - 1))
    return x2d.reshape(batch, seq_len, D_MODEL)


def _prepare(attn_norm, attn_w_in, attn_w_out, ssm_norm, ssm_w_in, ssm_a_re, ssm_a_im, ssm_log_step, ssm_b_re,
             ssm_b_im, ssm_c_re, ssm_c_im, ssm_d, ssm_w_glu, ssm_b_glu, ssm_w_out, final_norm):
    attn_params = []
    for j in range(attn_norm.shape[0]):
        attn_params.append((attn_norm[j][None, :], _prep_attn_w_in(attn_w_in[j]), attn_w_out[j].astype(BF16)))
    ssm_params = []
    for j in range(ssm_norm.shape[0]):
        mats = _prep_ssm(ssm_a_re[j], ssm_a_im[j], ssm_log_step[j], ssm_b_re[j], ssm_b_im[j],
                         ssm_c_re[j], ssm_c_im[j], ssm_d[j])
        ssm_params.append((ssm_norm[j][None, :], ssm_w_in[j].T.astype(BF16), mats,
                           ssm_w_glu[j].T.astype(BF16), ssm_b_glu[j][:, None], ssm_w_out[j].astype(BF16)))
    return attn_params, ssm_params, final_norm[None, :]


def kernel(x_prompt, x_sample, attn_norm, attn_w_in, attn_w_out, ssm_norm, ssm_w_in, ssm_a_re, ssm_a_im, ssm_log_step, ssm_b_re, ssm_b_im, ssm_c_re, ssm_c_im, ssm_d, ssm_w_glu, ssm_b_glu, ssm_w_out, final_norm):
    attn_params, ssm_params, final_gain = _prepare(
        attn_norm, attn_w_in, attn_w_out, ssm_norm, ssm_w_in, ssm_a_re, ssm_a_im, ssm_log_step, ssm_b_re,
        ssm_b_im, ssm_c_re, ssm_c_im, ssm_d, ssm_w_glu, ssm_b_glu, ssm_w_out, final_norm)
    y_prompt = _trunk(x_prompt, attn_params, ssm_params, final_gain)
    y_sample = _trunk(x_sample, attn_params, ssm_params, final_gain)
    return (y_prompt, y_sample)
```
